```python
import jax, jax.numpy as jnp
from jax import lax
import numpy as np

D_MODEL = 1024
BATCH = 2
SEQ = 16384
DEPTH = 4

HEAD_DIM = 64
N_TOK_HEADS = 12
N_KV_HEADS = 3
GQA_GROUP = N_TOK_HEADS // N_KV_HEADS
N_MEM_HEADS = 4
N_MEM = 256
Q_W = N_TOK_HEADS * HEAD_DIM
KV_W = N_KV_HEADS * HEAD_DIM
QM_W = N_MEM_HEADS * HEAD_DIM
IN_W = Q_W + 2 * KV_W + QM_W
MIX_WIDTH = Q_W + QM_W
D_FF = -(-8 * D_MODEL // (3 * 256)) * 256
N_MIXERS = 3
BLOCK = 128
A_RADIUS = 128
C_GROUPS = ((128, 1), (512, 4), (2048, 16))
ROPE_THETA = 500000.0
ROPE_DIMS = HEAD_DIM // 4
AXIAL_THETA = 10000.0
GRID_W = 64
EPS = 1e-6
N_A = (DEPTH + 2) // N_MIXERS
N_B = (DEPTH + 1) // N_MIXERS
ATTN_SCALE = HEAD_DIM ** -0.5

kernel_name = 'hybrid_interleaved_window_axial_dilated_encoder'


def rms_norm(x, g):
    xf = x.astype(jnp.float32)
    y = xf * lax.rsqrt(jnp.mean(xf * xf, axis=-1, keepdims=True) + EPS)
    return (y * g.astype(jnp.float32)).astype(x.dtype)


def rope_table(pos, n_dims, theta):
    inv = theta ** (-(jnp.arange(0, n_dims, 2, dtype=jnp.float32) / n_dims))
    ang = pos.astype(jnp.float32)[:, None] * inv[None, :]
    return jnp.cos(ang), jnp.sin(ang)


def apply_rotary(x, cos, sin):
    half = x.shape[-1] // 2
    xf = x.astype(jnp.float32)
    x1, x2 = xf[..., :half], xf[..., half:]
    c, s = cos[:, None, :], sin[:, None, :]
    return jnp.concatenate([x1 * c - x2 * s, x2 * c + x1 * s], axis=-1).astype(x.dtype)


def partial_rope(x, cos, sin):
    return jnp.concatenate([apply_rotary(x[..., :ROPE_DIMS], cos, sin), x[..., ROPE_DIMS:]], axis=-1)


def axial_rope(x, cos_r, sin_r, cos_c, sin_c):
    half = HEAD_DIM // 2
    return jnp.concatenate([apply_rotary(x[..., :half], cos_r, sin_r),
                            apply_rotary(x[..., half:], cos_c, sin_c)], axis=-1)


def banded_attention(q, k, v, radius, sink=None):
    B, L, KVH, G, HD = q.shape
    blk = radius
    nb = -(-L // blk)
    Lp = nb * blk
    pad = Lp - L
    qb = jnp.pad(q, [(0, 0), (0, pad), (0, 0), (0, 0), (0, 0)]).reshape(B, nb, blk, KVH, G, HD)

    def windows(t):
        tp = jnp.pad(t, [(0, 0), (blk, blk + pad), (0, 0), (0, 0)]).reshape(B, nb + 2, blk, KVH, HD)
        return jnp.concatenate([tp[:, :-2], tp[:, 1:-1], tp[:, 2:]], axis=2)

    kw, vw = windows(k), windows(v)
    qpos = jnp.arange(Lp).reshape(nb, blk)
    kpos = (jnp.arange(nb)[:, None] - 1) * blk + jnp.arange(3 * blk)[None, :]
    mask = ((jnp.abs(qpos[:, :, None] - kpos[:, None, :]) <= radius)
            & (kpos >= 0)[:, None, :] & (kpos < L)[:, None, :])
    s = jnp.einsum('bnqhgd,bnkhd->bnhgqk', qb, kw, preferred_element_type=jnp.float32) * ATTN_SCALE
    s = jnp.where(mask[None, :, None, None], s, -jnp.inf)
    m = jnp.max(s, axis=-1, keepdims=True)
    if sink is not None:
        sk = sink.astype(jnp.float32)[None, None, :, :, None, None]
        m = jnp.maximum(m, sk)
    p = jnp.exp(s - m)
    denom = jnp.sum(p, axis=-1)
    if sink is not None:
        denom = denom + jnp.exp(sk - m)[..., 0]
    o = jnp.einsum('bnhgqk,bnkhd->bnqhgd', p.astype(v.dtype), vw)
    den_t = jnp.transpose(denom, (0, 1, 4, 2, 3))
    o = (o / den_t[..., None]).astype(q.dtype).reshape(B, Lp, KVH, G, HD)[:, :L]
    lse = jnp.transpose(m[..., 0] + jnp.log(denom), (0, 1, 4, 2, 3)).reshape(B, Lp, KVH, G)[:, :L]
    return o, lse


def full_attention_blocks(q, k, v):
    B, S, KVH, G, HD = q.shape
    nb = S // BLOCK
    qb = jnp.moveaxis(q.reshape(B, nb, BLOCK, KVH, G, HD), 1, 0)

    def one_block(qblk):
        s = jnp.einsum('bqhgd,bkhd->bhgqk', qblk, k, preferred_element_type=jnp.float32) * ATTN_SCALE
        p = jax.nn.softmax(s, axis=-1)
        return jnp.einsum('bhgqk,bkhd->bqhgd', p.astype(v.dtype), v)

    o = lax.map(one_block, qb)
    return jnp.moveaxis(o, 0, 1).reshape(B, S, KVH, G, HD)


def mixer_a(q, k, v, sink, cos_p, sin_p):
    B, S = q.shape[:2]
    q = partial_rope(q, cos_p, sin_p).reshape(B, S, N_KV_HEADS, GQA_GROUP, HEAD_DIM)
    k = partial_rope(k, cos_p, sin_p)
    o, _ = banded_attention(q, k, v, A_RADIUS, sink.reshape(N_KV_HEADS, GQA_GROUP))
    return o.reshape(B, S, Q_W)


def mixer_b(q, k, v, qk_g, cos_r, sin_r, cos_c, sin_c):
    B, S = q.shape[:2]
    q = axial_rope(rms_norm(q, qk_g[0]), cos_r, sin_r, cos_c, sin_c)
    k = axial_rope(rms_norm(k, qk_g[1]), cos_r, sin_r, cos_c, sin_c)
    o = full_attention_blocks(q.reshape(B, S, N_KV_HEADS, GQA_GROUP, HEAD_DIM), k, v)
    return o.reshape(B, S, Q_W)


def dilated_group(q, k, v, dil, radius):
    B, S = q.shape[:2]
    L = S // dil

    def split(t):
        t = jnp.moveaxis(t.reshape((B, L, dil) + t.shape[2:]), 2, 1)
        return t.reshape((B * dil, L) + t.shape[3:])

    def merge(t):
        t = jnp.moveaxis(t.reshape((B, dil, L) + t.shape[2:]), 1, 2)
        return t.reshape((B, S) + t.shape[3:])

    o, lse = banded_attention(split(q)[:, :, None], split(k)[:, :, None], split(v)[:, :, None], radius)
    return merge(o[:, :, 0]), merge(lse[:, :, 0])


def mixer_c(q, k, v, cos_p, sin_p):
    B, S = q.shape[:2]
    q = partial_rope(q, cos_p, sin_p)
    k = partial_rope(k, cos_p, sin_p)
    outs, lses = [], []
    for g, (window, dil) in enumerate(C_GROUPS):
        o, l = dilated_group(q[:, :, g * GQA_GROUP:(g + 1) * GQA_GROUP], k[:, :, g], v[:, :, g],
                             dil, window // (2 * dil))
        outs.append(o)
        lses.append(l)
    alpha = jax.nn.softmax(jnp.stack(lses, axis=2), axis=2)
    o = jnp.stack(outs, axis=2) * alpha[..., None].astype(q.dtype)
    return o.reshape(B, S, Q_W)


def memory_attention(qm, km, vm):
    s = jnp.einsum('bshd,bmhd->bhsm', qm, km, preferred_element_type=jnp.float32) * ATTN_SCALE
    p = jax.nn.softmax(s, axis=-1)
    o = jnp.einsum('bhsm,bmhd->bshd', p.astype(vm.dtype), vm)
    return o.reshape(qm.shape[0], qm.shape[1], QM_W)


def setup_inputs(seed: int = 0) -> dict:
    key = jax.random.key(seed)
    ks = jax.random.split(key, 14)
    nrm = jax.random.normal
    f32 = jnp.float32
    return {
        'x': nrm(ks[0], (BATCH, SEQ, D_MODEL), f32),
        'mem': nrm(ks[1], (BATCH, N_MEM, D_MODEL), f32),
        'mem_norm_g': 1.0 + 0.02 * nrm(ks[2], (D_MODEL,), f32),
        'w_in': nrm(ks[3], (DEPTH, D_MODEL, IN_W), f32) * D_MODEL ** -0.5,
        'w_mem_kv': nrm(ks[4], (DEPTH, D_MODEL, 2 * QM_W), f32) * D_MODEL ** -0.5,
        'w_o': nrm(ks[5], (DEPTH, MIX_WIDTH, D_MODEL), f32) * MIX_WIDTH ** -0.5,
        'g_mix_pre': 1.0 + 0.02 * nrm(ks[6], (DEPTH, D_MODEL), f32),
        'g_mix_post': 1.0 + 0.02 * nrm(ks[7], (DEPTH, D_MODEL), f32),
        'attn_sink': 0.5 * nrm(ks[8], (N_A, N_TOK_HEADS), f32),
        'qk_norm_g': 1.0 + 0.02 * nrm(ks[9], (N_B, 2, HEAD_DIM), f32),
        'w_gate_up': nrm(ks[10], (DEPTH, D_MODEL, 2 * D_FF), f32) * D_MODEL ** -0.5,
        'w_down': nrm(ks[11], (DEPTH, D_FF, D_MODEL), f32) * D_FF ** -0.5,
        'g_ffn_pre': 1.0 + 0.02 * nrm(ks[12], (DEPTH, D_MODEL), f32),
        'g_ffn_post': 1.0 + 0.02 * nrm(ks[13], (DEPTH, D_MODEL), f32),
    }


def reference(x, mem, mem_norm_g, w_in, w_mem_kv, w_o, g_mix_pre, g_mix_post, attn_sink,
              qk_norm_g, w_gate_up, w_down, g_ffn_pre, g_ffn_post):
    B, S, _ = x.shape
    rows = S // GRID_W
    pos = jnp.arange(S, dtype=jnp.int32)
    row_ids = jnp.repeat(jnp.arange(rows, dtype=jnp.int32), GRID_W)
    col_ids = jnp.tile(jnp.arange(GRID_W, dtype=jnp.int32), rows)
    cos_p, sin_p = rope_table(pos, ROPE_DIMS, ROPE_THETA)
    cos_r, sin_r = rope_table(row_ids, HEAD_DIM // 2, AXIAL_THETA)
    cos_c, sin_c = rope_table(col_ids, HEAD_DIM // 2, AXIAL_THETA)
    mem_n = rms_norm(mem, mem_norm_g)

    for i in range(DEPTH):
        h = rms_norm(x, g_mix_pre[i])
        proj = h @ w_in[i]
        q = proj[..., :Q_W].reshape(B, S, N_TOK_HEADS, HEAD_DIM)
        k = proj[..., Q_W:Q_W + KV_W].reshape(B, S, N_KV_HEADS, HEAD_DIM)
        v = proj[..., Q_W + KV_W:Q_W + 2 * KV_W].reshape(B, S, N_KV_HEADS, HEAD_DIM)
        qm = proj[..., Q_W + 2 * KV_W:].reshape(B, S, N_MEM_HEADS, HEAD_DIM)
        kind = i % N_MIXERS
        if kind == 0:
            tok = mixer_a(q, k, v, attn_sink[i // N_MIXERS], cos_p, sin_p)
        elif kind == 1:
            tok = mixer_b(q, k, v, qk_norm_g[i // N_MIXERS], cos_r, sin_r, cos_c, sin_c)
        else:
            tok = mixer_c(q, k, v, cos_p, sin_p)
        mkv = mem_n @ w_mem_kv[i]
        km = mkv[..., :QM_W].reshape(B, N_MEM, N_MEM_HEADS, HEAD_DIM)
        vm = mkv[..., QM_W:].reshape(B, N_MEM, N_MEM_HEADS, HEAD_DIM)
        mo = memory_attention(qm, km, vm)
        o = jnp.concatenate([tok, mo], axis=-1) @ w_o[i]
        x = x + rms_norm(o, g_mix_post[i])

        h = rms_norm(x, g_ffn_pre[i])
        gu = h @ w_gate_up[i]
        f = (jax.nn.silu(gu[..., :D_FF]) * gu[..., D_FF:]) @ w_down[i]
        x = x + rms_norm(f, g_ffn_post[i])
    return x
```

```python
import functools

import numpy as np
import jax
import jax.numpy as jnp
from jax import lax
from jax.experimental import pallas as pl
from jax.experimental.pallas import tpu as pltpu

D_MODEL = 1024
DEPTH = 4
HEAD_DIM = 64
N_TOK_HEADS = 12
N_KV_HEADS = 3
GQA_GROUP = N_TOK_HEADS // N_KV_HEADS
N_MEM_HEADS = 4
N_MEM = 256
Q_W = N_TOK_HEADS * HEAD_DIM
KV_W = N_KV_HEADS * HEAD_DIM
QM_W = N_MEM_HEADS * HEAD_DIM
IN_W = Q_W + 2 * KV_W + QM_W
MIX_WIDTH = Q_W + QM_W
D_FF = -(-8 * D_MODEL // (3 * 256)) * 256
N_MIXERS = 3
A_RADIUS = 128
C_GROUPS = ((128, 1), (512, 4), (2048, 16))
C_RADIUS = 64
ROPE_THETA = 500000.0
ROPE_DIMS = HEAD_DIM // 4
AXIAL_THETA = 10000.0
GRID_W = 64
EPS = 1e-6
ATTN_SCALE = HEAD_DIM ** -0.5

F32 = jnp.float32
BF16 = jnp.bfloat16

TOKEN_TILE = 512
FF_TILE = 1408
ATTN_TILE = 512
A_SUB = 256
VMEM_LIMIT = 56 * 1024 * 1024

_TN = (((0,), (0,)), ((), ()))
_NT = (((1,), (1,)), ((), ()))


def _params(*semantics):
    return pltpu.CompilerParams(dimension_semantics=semantics, vmem_limit_bytes=VMEM_LIMIT)


def _rms_rows(x, g_col):
    ms = jnp.mean(x * x, axis=0, keepdims=True)
    return x * lax.rsqrt(ms + EPS) * g_col


def _transpose_kernel(x_ref, o_ref):
    o_ref[...] = x_ref[...].T


def _to_feature_major(x2d):
    t, d = x2d.shape
    return pl.pallas_call(
        _transpose_kernel,
        grid=(t // TOKEN_TILE,),
        in_specs=[pl.BlockSpec((TOKEN_TILE, d), lambda i: (i, 0))],
        out_specs=pl.BlockSpec((d, TOKEN_TILE), lambda i: (0, i)),
        out_shape=jax.ShapeDtypeStruct((d, t), x2d.dtype),
        compiler_params=_params("parallel"),
        name="to_feature_major",
    )(x2d)


def _to_token_major(xt):
    d, t = xt.shape
    return pl.pallas_call(
        _transpose_kernel,
        grid=(t // TOKEN_TILE,),
        in_specs=[pl.BlockSpec((d, TOKEN_TILE), lambda i: (0, i))],
        out_specs=pl.BlockSpec((TOKEN_TILE, d), lambda i: (i, 0)),
        out_shape=jax.ShapeDtypeStruct((t, d), xt.dtype),
        compiler_params=_params("parallel"),
        name="to_token_major",
    )(xt)


def _mem_kv_kernel(mem_ref, g_ref, wk_ref, wvt_ref, km_ref, vmt_ref):
    m = mem_ref[...]
    ms = jnp.mean(m * m, axis=-1, keepdims=True)
    mn = (m * lax.rsqrt(ms + EPS) * g_ref[...]).astype(BF16)
    for h in range(N_MEM_HEADS):
        km_ref[h] = jnp.dot(mn, wk_ref[h], preferred_element_type=F32).astype(BF16)
        vmt_ref[h] = lax.dot_general(wvt_ref[h], mn, _NT, preferred_element_type=F32).astype(BF16)


def _mem_kv(mem, g_row, wk, wvt):
    b = mem.shape[0]
    return pl.pallas_call(
        _mem_kv_kernel,
        grid=(DEPTH, b),
        in_specs=[
            pl.BlockSpec((None, N_MEM, D_MODEL), lambda l, i: (i, 0, 0)),
            pl.BlockSpec((1, D_MODEL), lambda l, i: (0, 0)),
            pl.BlockSpec((None, N_MEM_HEADS, D_MODEL, HEAD_DIM), lambda l, i: (l, 0, 0, 0)),
            pl.BlockSpec((None, N_MEM_HEADS, HEAD_DIM, D_MODEL), lambda l, i: (l, 0, 0, 0)),
        ],
        out_specs=[
            pl.BlockSpec((None, None, N_MEM_HEADS, N_MEM, HEAD_DIM), lambda l, i: (l, i, 0, 0, 0)),
            pl.BlockSpec((None, None, N_MEM_HEADS, HEAD_DIM, N_MEM), lambda l, i: (l, i, 0, 0, 0)),
        ],
        out_shape=[
            jax.ShapeDtypeStruct((DEPTH, b, N_MEM_HEADS, N_MEM, HEAD_DIM), BF16),
            jax.ShapeDtypeStruct((DEPTH, b, N_MEM_HEADS, HEAD_DIM, N_MEM), BF16),
        ],
        compiler_params=_params("parallel", "parallel"),
        name="mem_kv",
    )(mem, g_row, wk, wvt)


def _split_store(p, rot, q_ref, k_ref, v_ref, qm_ref):
    for hd in range(N_TOK_HEADS):
        r = hd * HEAD_DIM
        q_ref[r:r + HEAD_DIM, :] = (rot(p[r:r + HEAD_DIM], False) * ATTN_SCALE).astype(BF16)
    for hd in range(N_KV_HEADS):
        r = hd * HEAD_DIM
        k_ref[r:r + HEAD_DIM, :] = rot(p[Q_W + r:Q_W + r + HEAD_DIM], True).astype(BF16)
    v_ref[...] = p[Q_W + KV_W:Q_W + 2 * KV_W].astype(BF16)
    qm_ref[...] = (p[Q_W + 2 * KV_W:] * ATTN_SCALE).astype(BF16)


def _in_proj_partial_kernel(x_ref, g_ref, w_ref, cos_ref, sin_ref, q_ref, k_ref, v_ref, qm_ref):
    h = _rms_rows(x_ref[...], g_ref[...]).astype(BF16)
    p = jnp.dot(w_ref[...], h, preferred_element_type=F32)
    c, s = cos_ref[...], sin_ref[...]
    half = ROPE_DIMS // 2

    def rot(head, is_key):
        x1, x2 = head[:half], head[half:ROPE_DIMS]
        return jnp.concatenate([x1 * c - x2 * s, x2 * c + x1 * s, head[ROPE_DIMS:]], axis=0)

    _split_store(p, rot, q_ref, k_ref, v_ref, qm_ref)


def _in_proj_axial_kernel(x_ref, g_ref, w_ref, cr_ref, sr_ref, cc_ref, sc_ref, qkg_ref,
                          q_ref, k_ref, v_ref, qm_ref):
    h = _rms_rows(x_ref[...], g_ref[...]).astype(BF16)
    p = jnp.dot(w_ref[...], h, preferred_element_type=F32)
    cr, sr, cc, sc = cr_ref[...], sr_ref[...], cc_ref[...], sc_ref[...]
    qr = HEAD_DIM // 4

    def rot(head, is_key):
        y = _rms_rows(head, qkg_ref[1] if is_key else qkg_ref[0])
        a1, a2, b1, b2 = y[:qr], y[qr:2 * qr], y[2 * qr:3 * qr], y[3 * qr:]
        return jnp.concatenate(
            [a1 * cr - a2 * sr, a2 * cr + a1 * sr, b1 * cc - b2 * sc, b2 * cc + b1 * sc], axis=0)

    _split_store(p, rot, q_ref, k_ref, v_ref, qm_ref)


def _in_proj(xt, g_col, wt, tables, qk_g, seq_len):
    t = xt.shape[1]
    tm = TOKEN_TILE
    n_seq_tiles = seq_len // tm
    tab_specs = [pl.BlockSpec((tb.shape[0], tm), lambda i: (0, i % n_seq_tiles)) for tb in tables]
    in_specs = [
        pl.BlockSpec((D_MODEL, tm), lambda i: (0, i)),
        pl.BlockSpec((D_MODEL, 1), lambda i: (0, 0)),
        pl.BlockSpec((IN_W, D_MODEL), lambda i: (0, 0)),
    ] + tab_specs
    args = [xt, g_col, wt] + list(tables)
    if qk_g is None:
        body = _in_proj_partial_kernel
    else:
        body = _in_proj_axial_kernel
        in_specs.append(pl.BlockSpec((2, HEAD_DIM, 1), lambda i: (0, 0, 0)))
        args.append(qk_g)
    widths = (Q_W, KV_W, KV_W, QM_W)
    return pl.pallas_call(
        body,
        grid=(t // tm,),
        in_specs=in_specs,
        out_specs=[pl.BlockSpec((w, tm), lambda i: (0, i)) for w in widths],
        out_shape=[jax.ShapeDtypeStruct((w, t), BF16) for w in widths],
        compiler_params=_params("parallel"),
        name="in_proj",
    )(*args)


def _banded_kernel(sink_ref, q_ref, kl_ref, km_ref, kr_ref, vl_ref, vm_ref, vr_ref, o_ref, *, seq_len):
    g = pl.program_id(1)
    q0 = pl.program_id(2) * ATTN_TILE
    kwin = jnp.concatenate([kl_ref[...], km_ref[...], kr_ref[...]], axis=1)
    vwin = jnp.concatenate([vl_ref[...], vm_ref[...], vr_ref[...]], axis=1)
    win = A_SUB + 2 * A_RADIUS
    krow = lax.broadcasted_iota(jnp.int32, (win, A_SUB), 0)
    qcol = lax.broadcasted_iota(jnp.int32, (win, A_SUB), 1)
    band = (krow >= qcol) & (krow <= qcol + 2 * A_RADIUS)
    for u in range(ATTN_TILE // A_SUB):
        kpos = krow + (q0 + u * A_SUB - A_RADIUS)
        mask = band & (kpos >= 0) & (kpos < seq_len)
        ku = kwin[:, u * A_SUB:u * A_SUB + win]
        vu = vwin[:, u * A_SUB:u * A_SUB + win]
        for j in range(GQA_GROUP):
            r = j * HEAD_DIM
            qj = q_ref[r:r + HEAD_DIM, u * A_SUB:(u + 1) * A_SUB]
            s = lax.dot_general(ku, qj, _TN, preferred_element_type=F32)
            s = jnp.where(mask, s, -jnp.inf)
            sink = sink_ref[g * GQA_GROUP + j]
            m = jnp.maximum(jnp.max(s, axis=0, keepdims=True), sink)
            p = jnp.exp(s - m)
            den = jnp.sum(p, axis=0, keepdims=True) + jnp.exp(sink - m)
            o = jnp.dot(vu, p.astype(BF16), preferred_element_type=F32) / den
            o_ref[r:r + HEAD_DIM, u * A_SUB:(u + 1) * A_SUB] = o.astype(BF16)


def _banded_attention(sink, qt, kt, vt, batch, seq_len):
    tq = ATTN_TILE
    nq = seq_len // tq
    halo_per_tile = tq // A_RADIUS
    n_halo = seq_len // A_RADIUS

    def left(b, g, i):
        return (g, b * n_halo + jnp.maximum(i * halo_per_tile - 1, 0))

    def right(b, g, i):
        return (g, b * n_halo + jnp.minimum((i + 1) * halo_per_tile, n_halo - 1))

    def main(b, g, i):
        return (g, b * nq + i)

    halo_l = pl.BlockSpec((HEAD_DIM, A_RADIUS), left)
    halo_r = pl.BlockSpec((HEAD_DIM, A_RADIUS), right)
    mid = pl.BlockSpec((HEAD_DIM, tq), main)
    return pl.pallas_call(
        functools.partial(_banded_kernel, seq_len=seq_len),
        grid=(batch, N_KV_HEADS, nq),
        in_specs=[
            pl.BlockSpec(memory_space=pltpu.SMEM),
            pl.BlockSpec((GQA_GROUP * HEAD_DIM, tq), main),
            halo_l, mid, halo_r, halo_l, mid, halo_r,
        ],
        out_specs=pl.BlockSpec((GQA_GROUP * HEAD_DIM, tq), main),
        out_shape=jax.ShapeDtypeStruct(qt.shape, BF16),
        compiler_params=_params("parallel", "parallel", "parallel"),
        name="banded_attention",
    )(sink, qt, kt, kt, kt, vt, vt, vt)


def _flash_kernel(q_ref, k_ref, v_ref, o_ref, m_sc, l_sc, acc_sc):
    ki = pl.program_id(3)

    @pl.when(ki == 0)
    def _():
        m_sc[...] = jnp.full(m_sc.shape, -jnp.inf, F32)
        l_sc[...] = jnp.zeros(l_sc.shape, F32)
        acc_sc[...] = jnp.zeros(acc_sc.shape, F32)

    kt = k_ref[...]
    vt = v_ref[...]
    for j in range(GQA_GROUP):
        r = j * HEAD_DIM
        s = lax.dot_general(kt, q_ref[r:r + HEAD_DIM, :], _TN, preferred_element_type=F32)
        m_old = m_sc[j]
        m_new = jnp.maximum(m_old, jnp.max(s, axis=0, keepdims=True))
        alpha = jnp.exp(m_old - m_new)
        p = jnp.exp(s - m_new)
        l_sc[j] = alpha * l_sc[j] + jnp.sum(p, axis=0, keepdims=True)
        acc_sc[j] = alpha * acc_sc[j] + jnp.dot(vt, p.astype(BF16), preferred_element_type=F32)
        m_sc[j] = m_new

    @pl.when(ki == pl.num_programs(3) - 1)
    def _():
        for j in range(GQA_GROUP):
            r = j * HEAD_DIM
            o_ref[r:r + HEAD_DIM, :] = (acc_sc[j] / l_sc[j]).astype(BF16)


def _full_attention(qt, kt, vt, batch, seq_len):
    tq = tk = ATTN_TILE
    nq, nk = seq_len // tq, seq_len // tk
    return pl.pallas_call(
        _flash_kernel,
        grid=(batch, N_KV_HEADS, nq, nk),
        in_specs=[
            pl.BlockSpec((GQA_GROUP * HEAD_DIM, tq), lambda b, g, i, k: (g, b * nq + i)),
            pl.BlockSpec((HEAD_DIM, tk), lambda b, g, i, k: (g, b * nk + k)),
            pl.BlockSpec((HEAD_DIM, tk), lambda b, g, i, k: (g, b * nk + k)),
        ],
        out_specs=pl.BlockSpec((GQA_GROUP * HEAD_DIM, tq), lambda b, g, i, k: (g, b * nq + i)),
        out_shape=jax.ShapeDtypeStruct(qt.shape, BF16),
        scratch_shapes=[
            pltpu.VMEM((GQA_GROUP, 1, tq), F32),
            pltpu.VMEM((GQA_GROUP, 1, tq), F32),
            pltpu.VMEM((GQA_GROUP, HEAD_DIM, tq), F32),
        ],
        compiler_params=_params("parallel", "parallel", "parallel", "arbitrary"),
        name="full_attention",
    )(qt, kt, vt)


C_BLOCKS = 5


def _dilated_kernel(q_ref, *refs, seq_len):
    k_refs, v_refs, o_ref = refs[:C_BLOCKS], refs[C_BLOCKS:2 * C_BLOCKS], refs[2 * C_BLOCKS]
    tq = ATTN_TILE
    q0 = pl.program_id(1) * tq
    kwin = jnp.concatenate([r[...] for r in k_refs], axis=1)
    vwin = jnp.concatenate([r[...] for r in v_refs], axis=1)
    centre = (C_BLOCKS // 2) * tq
    ms, ls, accs = [], [], []
    for g, (window, dil) in enumerate(C_GROUPS):
        reach = C_RADIUS * dil
        lo = centre - -(-reach // 128) * 128
        hi = centre + tq + -(-reach // 128) * 128
        krow = lax.broadcasted_iota(jnp.int32, (hi - lo, tq), 0)
        qcol = lax.broadcasted_iota(jnp.int32, (hi - lo, tq), 1)
        delta = krow + (lo - centre) - qcol
        kpos = krow + (q0 + lo - centre)
        mask = (jnp.abs(delta) <= reach) & ((delta & (dil - 1)) == 0) & (kpos >= 0) & (kpos < seq_len)
        kg = kwin[g * HEAD_DIM:(g + 1) * HEAD_DIM, lo:hi]
        vg = vwin[g * HEAD_DIM:(g + 1) * HEAD_DIM, lo:hi]
        for j in range(GQA_GROUP):
            r = (g * GQA_GROUP + j) * HEAD_DIM
            s = lax.dot_general(kg, q_ref[r:r + HEAD_DIM, :], _TN, preferred_element_type=F32)
            s = jnp.where(mask, s, -jnp.inf)
            m = jnp.max(s, axis=0, keepdims=True)
            p = jnp.exp(s - m)
            ms.append(m)
            ls.append(jnp.sum(p, axis=0, keepdims=True))
            accs.append(jnp.dot(vg, p.astype(BF16), preferred_element_type=F32))
    n_groups = len(C_GROUPS)
    for j in range(GQA_GROUP):
        idx = [g * GQA_GROUP + j for g in range(n_groups)]
        mx = functools.reduce(jnp.maximum, [ms[i] for i in idx])
        es = [jnp.exp(ms[i] - mx) for i in idx]
        tot = functools.reduce(lambda a, b: a + b, [ls[i] * e for i, e in zip(idx, es)])
        for i, e in zip(idx, es):
            o_ref[i * HEAD_DIM:(i + 1) * HEAD_DIM, :] = (accs[i] * (e / tot)).astype(BF16)


def _dilated_attention(qt, kt, vt, batch, seq_len):
    tq = ATTN_TILE
    nq = seq_len // tq

    def block(off):
        return pl.BlockSpec((KV_W, tq), lambda b, i: (0, b * nq + jnp.clip(i + off, 0, nq - 1)))

    offs = range(-(C_BLOCKS // 2), C_BLOCKS // 2 + 1)
    return pl.pallas_call(
        functools.partial(_dilated_kernel, seq_len=seq_len),
        grid=(batch, nq),
        in_specs=[pl.BlockSpec((Q_W, tq), lambda b, i: (0, b * nq + i))]
        + [block(o) for o in offs] + [block(o) for o in offs],
        out_specs=pl.BlockSpec((Q_W, tq), lambda b, i: (0, b * nq + i)),
        out_shape=jax.ShapeDtypeStruct(qt.shape, BF16),
        compiler_params=_params("parallel", "parallel"),
        name="dilated_attention",
    )(qt, *([kt] * C_BLOCKS), *([vt] * C_BLOCKS))


def _out_proj_kernel(x_ref, tok_ref, qm_ref, km_ref, vmt_ref, w_ref, g_ref, o_ref):
    mos = []
    for h in range(N_MEM_HEADS):
        r = h * HEAD_DIM
        s = jnp.dot(km_ref[h], qm_ref[r:r + HEAD_DIM, :], preferred_element_type=F32)
        m = jnp.max(s, axis=0, keepdims=True)
        p = jnp.exp(s - m)
        den = jnp.sum(p, axis=0, keepdims=True)
        o = jnp.dot(vmt_ref[h], p.astype(BF16), preferred_element_type=F32) / den
        mos.append(o.astype(BF16))
    mo = jnp.concatenate(mos, axis=0)
    y = (jnp.dot(w_ref[:, :Q_W], tok_ref[...], preferred_element_type=F32)
         + jnp.dot(w_ref[:, Q_W:], mo, preferred_element_type=F32))
    o_ref[...] = x_ref[...] + _rms_rows(y, g_ref[...])


def _out_proj(xt, tok_t, qm_t, km, vmt, wt, g_col, seq_len):
    t = xt.shape[1]
    tm = TOKEN_TILE
    per_seq = seq_len // tm
    return pl.pallas_call(
        _out_proj_kernel,
        grid=(t // tm,),
        in_specs=[
            pl.BlockSpec((D_MODEL, tm), lambda i: (0, i)),
            pl.BlockSpec((Q_W, tm), lambda i: (0, i)),
            pl.BlockSpec((QM_W, tm), lambda i: (0, i)),
            pl.BlockSpec((None, N_MEM_HEADS, N_MEM, HEAD_DIM), lambda i: (i // per_seq, 0, 0, 0)),
            pl.BlockSpec((None, N_MEM_HEADS, HEAD_DIM, N_MEM), lambda i: (i // per_seq, 0, 0, 0)),
            pl.BlockSpec((D_MODEL, MIX_WIDTH), lambda i: (0, 0)),
            pl.BlockSpec((D_MODEL, 1), lambda i: (0, 0)),
        ],
        out_specs=pl.BlockSpec((D_MODEL, tm), lambda i: (0, i)),
        out_shape=jax.ShapeDtypeStruct(xt.shape, F32),
        compiler_params=_params("parallel"),
        name="out_proj",
    )(xt, tok_t, qm_t, km, vmt, wt, g_col)


def _ffn_kernel(x_ref, gpre_ref, wg_ref, wu_ref, wd_ref, gpost_ref, o_ref, h_sc, acc_sc):
    j = pl.program_id(1)

    @pl.when(j == 0)
    def _():
        h_sc[...] = _rms_rows(x_ref[...], gpre_ref[...]).astype(BF16)
        acc_sc[...] = jnp.zeros(acc_sc.shape, F32)

    h = h_sc[...]
    gate = jnp.dot(wg_ref[...], h, preferred_element_type=F32)
    up = jnp.dot(wu_ref[...], h, preferred_element_type=F32)
    act = (gate * jax.nn.sigmoid(gate) * up).astype(BF16)
    acc_sc[...] += jnp.dot(wd_ref[...], act, preferred_element_type=F32)

    @pl.when(j == pl.num_programs(1) - 1)
    def _():
        o_ref[...] = x_ref[...] + _rms_rows(acc_sc[...], gpost_ref[...])


def _ffn(xt, gpre, wg_t, wu_t, wd_t, gpost):
    t = xt.shape[1]
    tm, tf = TOKEN_TILE, FF_TILE
    return pl.pallas_call(
        _ffn_kernel,
        grid=(t // tm, D_FF // tf),
        in_specs=[
            pl.BlockSpec((D_MODEL, tm), lambda i, j: (0, i)),
            pl.BlockSpec((D_MODEL, 1), lambda i, j: (0, 0)),
            pl.BlockSpec((tf, D_MODEL), lambda i, j: (j, 0)),
            pl.BlockSpec((tf, D_MODEL), lambda i, j: (j, 0)),
            pl.BlockSpec((D_MODEL, tf), lambda i, j: (0, j)),
            pl.BlockSpec((D_MODEL, 1), lambda i, j: (0, 0)),
        ],
        out_specs=pl.BlockSpec((D_MODEL, tm), lambda i, j: (0, i)),
        out_shape=jax.ShapeDtypeStruct(xt.shape, F32),
        scratch_shapes=[pltpu.VMEM((D_MODEL, tm), BF16), pltpu.VMEM((D_MODEL, tm), F32)],
        compiler_params=_params("parallel", "arbitrary"),
        name="ffn",
    )(xt, gpre, wg_t, wu_t, wd_t, gpost)


def _rope_table_t(pos, n_dims, theta):
    inv = theta ** (-(jnp.arange(0, n_dims, 2, dtype=F32) / n_dims))
    ang = inv[:, None] * pos.astype(F32)[None, :]
    return jnp.cos(ang), jnp.sin(ang)


def kernel(x, mem, mem_norm_g, w_in, w_mem_kv, w_o, g_mix_pre, g_mix_post, attn_sink, qk_norm_g,
           w_gate_up, w_down, g_ffn_pre, g_ffn_post):
    batch, seq_len, _ = x.shape
    assert seq_len % ATTN_TILE == 0 and seq_len % TOKEN_TILE == 0 and D_FF % FF_TILE == 0
    pos = jnp.arange(seq_len, dtype=jnp.int32)
    partial_tables = _rope_table_t(pos, ROPE_DIMS, ROPE_THETA)
    axial_tables = (_rope_table_t(pos // GRID_W, HEAD_DIM // 2, AXIAL_THETA)
                    + _rope_table_t(pos % GRID_W, HEAD_DIM // 2, AXIAL_THETA))

    w_in_t = jnp.swapaxes(w_in, 1, 2).astype(BF16)
    w_o_t = jnp.swapaxes(w_o, 1, 2).astype(BF16)
    w_gu_t = jnp.swapaxes(w_gate_up, 1, 2).astype(BF16)
    w_g_t, w_u_t = w_gu_t[:, :D_FF], w_gu_t[:, D_FF:]
    w_d_t = jnp.swapaxes(w_down, 1, 2).astype(BF16)
    w_mem = w_mem_kv.astype(BF16).reshape(DEPTH, D_MODEL, 2, N_MEM_HEADS, HEAD_DIM)
    w_mk = jnp.transpose(w_mem[:, :, 0], (0, 2, 1, 3))
    w_mv_t = jnp.transpose(w_mem[:, :, 1], (0, 2, 3, 1))

    km, vmt = _mem_kv(mem, mem_norm_g.reshape(1, D_MODEL), w_mk, w_mv_t)

    xt = _to_feature_major(x.reshape(batch * seq_len, D_MODEL))
    for i in range(DEPTH):
        kind = i % N_MIXERS
        g_pre = g_mix_pre[i].reshape(D_MODEL, 1)
        if kind == 1:
            qk_g = qk_norm_g[i // N_MIXERS].reshape(2, HEAD_DIM, 1)
            qt, kt, vt, qmt = _in_proj(xt, g_pre, w_in_t[i], axial_tables, qk_g, seq_len)
            tok_t = _full_attention(qt, kt, vt, batch, seq_len)
        else:
            qt, kt, vt, qmt = _in_proj(xt, g_pre, w_in_t[i], partial_tables, None, seq_len)
            if kind == 0:
                tok_t = _banded_attention(attn_sink[i // N_MIXERS], qt, kt, vt, batch, seq_len)
            else:
                tok_t = _dilated_attention(qt, kt, vt, batch, seq_len)
        xt = _out_proj(xt, tok_t, qmt, km[i], vmt[i], w_o_t[i], g_mix_post[i].reshape(D_MODEL, 1), seq_len)
        xt = _ffn(xt, g_ffn_pre[i].reshape(D_MODEL, 1), w_g_t[i], w_u_t[i], w_d_t[i],
                  g_ffn_post[i].reshape(D_MODEL, 1))
    return _to_token_major(xt).reshape(batch, seq_len, D_MODEL)
```

```python
import functools

import numpy as np
import jax
import jax.numpy as jnp
from jax import lax
from jax.experimental import pallas as pl
from jax.experimental.pallas import tpu as pltpu

D_MODEL = 1024
DEPTH = 4
HEAD_DIM = 64
N_TOK_HEADS = 12
N_KV_HEADS = 3
GQA_GROUP = N_TOK_HEADS // N_KV_HEADS
N_MEM_HEADS = 4
N_MEM = 256
Q_W = N_TOK_HEADS * HEAD_DIM
KV_W = N_KV_HEADS * HEAD_DIM
QM_W = N_MEM_HEADS * HEAD_DIM
IN_W = Q_W + 2 * KV_W + QM_W
MIX_WIDTH = Q_W + QM_W
D_FF = -(-8 * D_MODEL // (3 * 256)) * 256
N_MIXERS = 3
A_RADIUS = 128
C_GROUPS = ((128, 1), (512, 4), (2048, 16))
C_RADIUS = 64
ROPE_THETA = 500000.0
ROPE_DIMS = HEAD_DIM // 4
AXIAL_THETA = 10000.0
GRID_W = 64
EPS = 1e-6
ATTN_SCALE = HEAD_DIM ** -0.5
LOG2E = 1.4426950408889634
Q_SCALE = ATTN_SCALE * LOG2E

F32 = jnp.float32
BF16 = jnp.bfloat16

TOKEN_TILE = 512
FF_TILE = 1408
ATTN_TILE = 512
Q_SUB = 256
KEY_CHUNK = 256
FLASH_CHUNKS = 4
UNIT_LOOKAHEAD = 5
VMEM_LIMIT = 56 * 1024 * 1024

_TN = (((0,), (0,)), ((), ()))
_NT = (((1,), (1,)), ((), ()))


def _params(*semantics):
    return pltpu.CompilerParams(dimension_semantics=semantics, vmem_limit_bytes=VMEM_LIMIT)


def _rms_rows(x, g_col):
    ms = jnp.mean(x * x, axis=0, keepdims=True)
    return x * lax.rsqrt(ms + EPS) * g_col


def _transpose_kernel(x_ref, o_ref):
    o_ref[...] = x_ref[...].T


def _to_feature_major(x2d):
    t, d = x2d.shape
    return pl.pallas_call(
        _transpose_kernel,
        grid=(t // TOKEN_TILE,),
        in_specs=[pl.BlockSpec((TOKEN_TILE, d), lambda i: (i, 0))],
        out_specs=pl.BlockSpec((d, TOKEN_TILE), lambda i: (0, i)),
        out_shape=jax.ShapeDtypeStruct((d, t), x2d.dtype),
        compiler_params=_params("parallel"),
        name="to_feature_major",
    )(x2d)


def _to_token_major(xt):
    d, t = xt.shape
    return pl.pallas_call(
        _transpose_kernel,
        grid=(t // TOKEN_TILE,),
        in_specs=[pl.BlockSpec((d, TOKEN_TILE), lambda i: (0, i))],
        out_specs=pl.BlockSpec((TOKEN_TILE, d), lambda i: (i, 0)),
        out_shape=jax.ShapeDtypeStruct((t, d), xt.dtype),
        compiler_params=_params("parallel"),
        name="to_token_major",
    )(xt)


def _mem_kv_kernel(mem_ref, g_ref, wk_ref, wvt_ref, km_ref, vmt_ref):
    m = mem_ref[...]
    ms = jnp.mean(m * m, axis=-1, keepdims=True)
    mn = (m * lax.rsqrt(ms + EPS) * g_ref[...]).astype(BF16)
    for h in range(N_MEM_HEADS):
        km_ref[h] = jnp.dot(mn, wk_ref[h], preferred_element_type=F32).astype(BF16)
        vmt_ref[h] = lax.dot_general(wvt_ref[h], mn, _NT, preferred_element_type=F32).astype(BF16)


def _mem_kv(mem, g_row, wk, wvt):
    b = mem.shape[0]
    return pl.pallas_call(
        _mem_kv_kernel,
        grid=(DEPTH, b),
        in_specs=[
            pl.BlockSpec((None, N_MEM, D_MODEL), lambda l, i: (i, 0, 0)),
            pl.BlockSpec((1, D_MODEL), lambda l, i: (0, 0)),
            pl.BlockSpec((None, N_MEM_HEADS, D_MODEL, HEAD_DIM), lambda l, i: (l, 0, 0, 0)),
            pl.BlockSpec((None, N_MEM_HEADS, HEAD_DIM, D_MODEL), lambda l, i: (l, 0, 0, 0)),
        ],
        out_specs=[
            pl.BlockSpec((None, None, N_MEM_HEADS, N_MEM, HEAD_DIM), lambda l, i: (l, i, 0, 0, 0)),
            pl.BlockSpec((None, None, N_MEM_HEADS, HEAD_DIM, N_MEM), lambda l, i: (l, i, 0, 0, 0)),
        ],
        out_shape=[
            jax.ShapeDtypeStruct((DEPTH, b, N_MEM_HEADS, N_MEM, HEAD_DIM), BF16),
            jax.ShapeDtypeStruct((DEPTH, b, N_MEM_HEADS, HEAD_DIM, N_MEM), BF16),
        ],
        compiler_params=_params("parallel", "parallel"),
        name="mem_kv",
    )(mem, g_row, wk, wvt)


def _split_store(p, rot, q_ref, k_ref, v_ref, qm_ref):
    for hd in range(N_TOK_HEADS):
        r = hd * HEAD_DIM
        q_ref[r:r + HEAD_DIM, :] = (rot(p[r:r + HEAD_DIM], False) * Q_SCALE).astype(BF16)
    for hd in range(N_KV_HEADS):
        r = hd * HEAD_DIM
        k_ref[r:r + HEAD_DIM, :] = rot(p[Q_W + r:Q_W + r + HEAD_DIM], True).astype(BF16)
    v_ref[...] = p[Q_W + KV_W:Q_W + 2 * KV_W].astype(BF16)
    qm_ref[...] = (p[Q_W + 2 * KV_W:] * Q_SCALE).astype(BF16)


def _in_proj_partial_kernel(x_ref, g_ref, w_ref, cos_ref, sin_ref, q_ref, k_ref, v_ref, qm_ref):
    h = _rms_rows(x_ref[...], g_ref[...]).astype(BF16)
    p = jnp.dot(w_ref[...], h, preferred_element_type=F32)
    c, s = cos_ref[...], sin_ref[...]
    half = ROPE_DIMS // 2

    def rot(head, is_key):
        x1, x2 = head[:half], head[half:ROPE_DIMS]
        return jnp.concatenate([x1 * c - x2 * s, x2 * c + x1 * s, head[ROPE_DIMS:]], axis=0)

    _split_store(p, rot, q_ref, k_ref, v_ref, qm_ref)


def _in_proj_axial_kernel(x_ref, g_ref, w_ref, cr_ref, sr_ref, cc_ref, sc_ref, qkg_ref,
                          q_ref, k_ref, v_ref, qm_ref):
    h = _rms_rows(x_ref[...], g_ref[...]).astype(BF16)
    p = jnp.dot(w_ref[...], h, preferred_element_type=F32)
    cr, sr, cc, sc = cr_ref[...], sr_ref[...], cc_ref[...], sc_ref[...]
    qr = HEAD_DIM // 4

    def rot(head, is_key):
        y = _rms_rows(head, qkg_ref[1] if is_key else qkg_ref[0])
        a1, a2, b1, b2 = y[:qr], y[qr:2 * qr], y[2 * qr:3 * qr], y[3 * qr:]
        return jnp.concatenate(
            [a1 * cr - a2 * sr, a2 * cr + a1 * sr, b1 * cc - b2 * sc, b2 * cc + b1 * sc], axis=0)

    _split_store(p, rot, q_ref, k_ref, v_ref, qm_ref)


def _in_proj(xt, g_col, wt, tables, qk_g, seq_len):
    t = xt.shape[1]
    tm = TOKEN_TILE
    n_seq_tiles = seq_len // tm
    tab_specs = [pl.BlockSpec((tb.shape[0], tm), lambda i: (0, i % n_seq_tiles)) for tb in tables]
    in_specs = [
        pl.BlockSpec((D_MODEL, tm), lambda i: (0, i)),
        pl.BlockSpec((D_MODEL, 1), lambda i: (0, 0)),
        pl.BlockSpec((IN_W, D_MODEL), lambda i: (0, 0)),
    ] + tab_specs
    args = [xt, g_col, wt] + list(tables)
    if qk_g is None:
        body = _in_proj_partial_kernel
    else:
        body = _in_proj_axial_kernel
        in_specs.append(pl.BlockSpec((2, HEAD_DIM, 1), lambda i: (0, 0, 0)))
        args.append(qk_g)
    widths = (Q_W, KV_W, KV_W, QM_W)
    return pl.pallas_call(
        body,
        grid=(t // tm,),
        in_specs=in_specs,
        out_specs=[pl.BlockSpec((w, tm), lambda i: (0, i)) for w in widths],
        out_shape=[jax.ShapeDtypeStruct((w, t), BF16) for w in widths],
        compiler_params=_params("parallel"),
        name="in_proj",
    )(*args)


def _sublane_all(x, op):
    for shift in (4, 2, 1):
        x = op(x, pltpu.roll(x, shift, 0))
    return x


def _scores(kc, q, mask=None):
    s = jnp.dot(kc, q, preferred_element_type=F32)
    if mask is not None:
        s = jnp.where(mask, s, -jnp.inf)
    return s.reshape(kc.shape[0] // 8, 8, q.shape[1])


def _softmax_update(s3, vc, m_old, l_old, acc_old):
    n, _, qs = s3.shape
    m_new = jnp.maximum(m_old, _sublane_all(jnp.max(s3, axis=0), jnp.maximum))
    alpha = jnp.exp2(m_old - m_new)
    p3 = jnp.exp2(s3 - m_new[None])
    l_new = alpha * l_old + jnp.sum(p3, axis=0)
    pv = jnp.dot(vc, p3.reshape(n * 8, qs).astype(BF16), preferred_element_type=F32)
    acc3 = alpha[None] * acc_old.reshape(HEAD_DIM // 8, 8, qs) + pv.reshape(HEAD_DIM // 8, 8, qs)
    return m_new, l_new, acc3.reshape(HEAD_DIM, qs)


def _run_units(units, score_fn, update_fn, lookahead):
    pending = {}
    for n in range(len(units) + lookahead):
        if n < len(units):
            pending[n] = score_fn(units[n])
        if n >= lookahead:
            update_fn(units[n - lookahead], pending.pop(n - lookahead))


def _normalise(acc, l):
    qs = acc.shape[1]
    tot = _sublane_all(l, jnp.add)
    return (acc.reshape(HEAD_DIM // 8, 8, qs) / tot[None]).reshape(HEAD_DIM, qs)


def _banded_kernel(sink_ref, q_ref, kl_ref, km_ref, kr_ref, vl_ref, vm_ref, vr_ref, o_ref, *, seq_len):
    g = pl.program_id(1)
    q0 = pl.program_id(2) * ATTN_TILE
    kwin = jnp.concatenate([kl_ref[...], km_ref[...], kr_ref[...]], axis=1).T
    vwin = jnp.concatenate([vl_ref[...], vm_ref[...], vr_ref[...]], axis=1)
    n_sub = ATTN_TILE // Q_SUB
    n_chunk = (Q_SUB + 2 * A_RADIUS) // KEY_CHUNK
    krow = lax.broadcasted_iota(jnp.int32, (KEY_CHUNK, Q_SUB), 0)
    qcol = lax.broadcasted_iota(jnp.int32, (KEY_CHUNK, Q_SUB), 1)
    masks = {}
    for c in range(n_chunk):
        delta = krow - qcol + (c * KEY_CHUNK - A_RADIUS)
        band = jnp.abs(delta) <= A_RADIUS
        for u in range(n_sub):
            kpos = krow + (q0 - A_RADIUS + u * Q_SUB + c * KEY_CHUNK)
            masks[u, c] = band & (kpos >= 0) & (kpos < seq_len)

    first_sublane = lax.broadcasted_iota(jnp.int32, (8, Q_SUB), 0) == 0
    state = {}
    for j in range(GQA_GROUP):
        sink = sink_ref[g * GQA_GROUP + j] * LOG2E
        for u in range(n_sub):
            state[j, u] = (jnp.full((8, Q_SUB), sink, F32), jnp.where(first_sublane, 1.0, 0.0),
                           jnp.zeros((HEAD_DIM, Q_SUB), F32))

    def scores(unit):
        u, c, j = unit
        w0 = u * Q_SUB + c * KEY_CHUNK
        q = q_ref[j * HEAD_DIM:(j + 1) * HEAD_DIM, u * Q_SUB:(u + 1) * Q_SUB]
        return _scores(kwin[w0:w0 + KEY_CHUNK], q, masks[u, c])

    def update(unit, s3):
        u, c, j = unit
        w0 = u * Q_SUB + c * KEY_CHUNK
        state[j, u] = _softmax_update(s3, vwin[:, w0:w0 + KEY_CHUNK], *state[j, u])

    units = [(u, c, j) for u in range(n_sub) for c in range(n_chunk) for j in range(GQA_GROUP)]
    _run_units(units, scores, update, UNIT_LOOKAHEAD)

    for (j, u), (_, l, acc) in state.items():
        o_ref[j * HEAD_DIM:(j + 1) * HEAD_DIM, u * Q_SUB:(u + 1) * Q_SUB] = _normalise(acc, l).astype(BF16)


def _banded_attention(sink, qt, kt, vt, batch, seq_len):
    tq = ATTN_TILE
    nq = seq_len // tq
    halo_per_tile = tq // A_RADIUS
    n_halo = seq_len // A_RADIUS

    def left(b, g, i):
        return (g, b * n_halo + jnp.maximum(i * halo_per_tile - 1, 0))

    def right(b, g, i):
        return (g, b * n_halo + jnp.minimum((i + 1) * halo_per_tile, n_halo - 1))

    def main(b, g, i):
        return (g, b * nq + i)

    halo_l = pl.BlockSpec((HEAD_DIM, A_RADIUS), left)
    halo_r = pl.BlockSpec((HEAD_DIM, A_RADIUS), right)
    mid = pl.BlockSpec((HEAD_DIM, tq), main)
    return pl.pallas_call(
        functools.partial(_banded_kernel, seq_len=seq_len),
        grid=(batch, N_KV_HEADS, nq),
        in_specs=[
            pl.BlockSpec(memory_space=pltpu.SMEM),
            pl.BlockSpec((GQA_GROUP * HEAD_DIM, tq), main),
            halo_l, mid, halo_r, halo_l, mid, halo_r,
        ],
        out_specs=pl.BlockSpec((GQA_GROUP * HEAD_DIM, tq), main),
        out_shape=jax.ShapeDtypeStruct(qt.shape, BF16),
        compiler_params=_params("parallel", "parallel", "parallel"),
        name="banded_attention",
    )(sink, qt, kt, kt, kt, vt, vt, vt)


def _flash_kernel(q_ref, k_ref, v_ref, o_ref, k_sc, m_sc, l_sc, acc_sc, *, seq_len):
    tq = q_ref.shape[1]

    @pl.when(pl.program_id(2) == 0)
    def _():
        def fill(c, carry):
            off = pl.multiple_of(c * ATTN_TILE, ATTN_TILE)
            k_sc[pl.ds(off, ATTN_TILE), :] = k_ref[:, pl.ds(off, ATTN_TILE)].T
            return carry
        lax.fori_loop(0, seq_len // ATTN_TILE, fill, 0)

    m_sc[...] = jnp.full(m_sc.shape, -jnp.inf, F32)
    l_sc[...] = jnp.zeros(l_sc.shape, F32)
    acc_sc[...] = jnp.zeros(acc_sc.shape, F32)

    subs = [(j, u) for j in range(GQA_GROUP) for u in range(tq // Q_SUB)]
    units = [(c, j, u) for c in range(FLASH_CHUNKS) for (j, u) in subs]

    def body(it, carry):
        kcs, vcs = [], []
        for c in range(FLASH_CHUNKS):
            off = pl.multiple_of((it * FLASH_CHUNKS + c) * KEY_CHUNK, KEY_CHUNK)
            kcs.append(k_sc[pl.ds(off, KEY_CHUNK), :])
            vcs.append(v_ref[:, pl.ds(off, KEY_CHUNK)])
        state = {}
        for j, u in subs:
            cols = slice(u * Q_SUB, (u + 1) * Q_SUB)
            state[j, u] = (m_sc[j, :, cols], l_sc[j, :, cols], acc_sc[j, :, cols])

        def scores(unit):
            c, j, u = unit
            return _scores(kcs[c], q_ref[j * HEAD_DIM:(j + 1) * HEAD_DIM, u * Q_SUB:(u + 1) * Q_SUB])

        def update(unit, s3):
            c, j, u = unit
            state[j, u] = _softmax_update(s3, vcs[c], *state[j, u])

        _run_units(units, scores, update, UNIT_LOOKAHEAD)
        for j, u in subs:
            cols = slice(u * Q_SUB, (u + 1) * Q_SUB)
            m_sc[j, :, cols], l_sc[j, :, cols], acc_sc[j, :, cols] = state[j, u]
        return carry

    lax.fori_loop(0, seq_len // (KEY_CHUNK * FLASH_CHUNKS), body, 0)

    for j in range(GQA_GROUP):
        o_ref[j * HEAD_DIM:(j + 1) * HEAD_DIM, :] = _normalise(acc_sc[j], l_sc[j]).astype(BF16)


def _full_attention(qt, kt, vt, batch, seq_len):
    tq = ATTN_TILE
    nq = seq_len // tq
    return pl.pallas_call(
        functools.partial(_flash_kernel, seq_len=seq_len),
        grid=(batch, N_KV_HEADS, nq),
        in_specs=[
            pl.BlockSpec((GQA_GROUP * HEAD_DIM, tq), lambda b, g, i: (g, b * nq + i)),
            pl.BlockSpec((HEAD_DIM, seq_len), lambda b, g, i: (g, b)),
            pl.BlockSpec((HEAD_DIM, seq_len), lambda b, g, i: (g, b)),
        ],
        out_specs=pl.BlockSpec((GQA_GROUP * HEAD_DIM, tq), lambda b, g, i: (g, b * nq + i)),
        out_shape=jax.ShapeDtypeStruct(qt.shape, BF16),
        scratch_shapes=[
            pltpu.VMEM((seq_len, HEAD_DIM), BF16),
            pltpu.VMEM((GQA_GROUP, 8, tq), F32),
            pltpu.VMEM((GQA_GROUP, 8, tq), F32),
            pltpu.VMEM((GQA_GROUP, HEAD_DIM, tq), F32),
        ],
        compiler_params=_params("parallel", "parallel", "arbitrary"),
        name="full_attention",
    )(qt, kt, vt)


C_BLOCKS = 5


def _dilated_kernel(q_ref, *refs, seq_len):
    k_refs, v_refs, o_ref = refs[:C_BLOCKS], refs[C_BLOCKS:2 * C_BLOCKS], refs[2 * C_BLOCKS]
    tq = ATTN_TILE
    q0 = pl.program_id(1) * tq
    kwin = jnp.concatenate([r[...] for r in k_refs], axis=1)
    vwin = jnp.concatenate([r[...] for r in v_refs], axis=1)
    centre = (C_BLOCKS // 2) * tq
    ms, ls, accs = [], [], []
    for g, (window, dil) in enumerate(C_GROUPS):
        reach = C_RADIUS * dil
        lo = centre - -(-reach // 128) * 128
        hi = centre + tq + -(-reach // 128) * 128
        krow = lax.broadcasted_iota(jnp.int32, (hi - lo, tq), 0)
        qcol = lax.broadcasted_iota(jnp.int32, (hi - lo, tq), 1)
        delta = krow + (lo - centre) - qcol
        kpos = krow + (q0 + lo - centre)
        mask = (jnp.abs(delta) <= reach) & ((delta & (dil - 1)) == 0) & (kpos >= 0) & (kpos < seq_len)
        kg = kwin[g * HEAD_DIM:(g + 1) * HEAD_DIM, lo:hi]
        vg = vwin[g * HEAD_DIM:(g + 1) * HEAD_DIM, lo:hi]
        for j in range(GQA_GROUP):
            r = (g * GQA_GROUP + j) * HEAD_DIM
            s = lax.dot_general(kg, q_ref[r:r + HEAD_DIM, :], _TN, preferred_element_type=F32)
            s = jnp.where(mask, s, -jnp.inf)
            m = jnp.max(s, axis=0, keepdims=True)
            p = jnp.exp2(s - m)
            ms.append(m)
            ls.append(jnp.sum(p, axis=0, keepdims=True))
            accs.append(jnp.dot(vg, p.astype(BF16), preferred_element_type=F32))
    n_groups = len(C_GROUPS)
    for j in range(GQA_GROUP):
        idx = [g * GQA_GROUP + j for g in range(n_groups)]
        mx = functools.reduce(jnp.maximum, [ms[i] for i in idx])
        es = [jnp.exp2(ms[i] - mx) for i in idx]
        tot = functools.reduce(lambda a, b: a + b, [ls[i] * e for i, e in zip(idx, es)])
        for i, e in zip(idx, es):
            o_ref[i * HEAD_DIM:(i + 1) * HEAD_DIM, :] = (accs[i] * (e / tot)).astype(BF16)


def _dilated_attention(qt, kt, vt, batch, seq_len):
    tq = ATTN_TILE
    nq = seq_len // tq

    def block(off):
        return pl.BlockSpec((KV_W, tq), lambda b, i: (0, b * nq + jnp.clip(i + off, 0, nq - 1)))

    offs = range(-(C_BLOCKS // 2), C_BLOCKS // 2 + 1)
    return pl.pallas_call(
        functools.partial(_dilated_kernel, seq_len=seq_len),
        grid=(batch, nq),
        in_specs=[pl.BlockSpec((Q_W, tq), lambda b, i: (0, b * nq + i))]
        + [block(o) for o in offs] + [block(o) for o in offs],
        out_specs=pl.BlockSpec((Q_W, tq), lambda b, i: (0, b * nq + i)),
        out_shape=jax.ShapeDtypeStruct(qt.shape, BF16),
        compiler_params=_params("parallel", "parallel"),
        name="dilated_attention",
    )(qt, *([kt] * C_BLOCKS), *([vt] * C_BLOCKS))


def _out_proj_kernel(x_ref, tok_ref, qm_ref, km_ref, vmt_ref, w_ref, g_ref, o_ref):
    tm = x_ref.shape[1]
    fresh = (jnp.full((8, Q_SUB), -jnp.inf, F32), jnp.zeros((8, Q_SUB), F32),
             jnp.zeros((HEAD_DIM, Q_SUB), F32))
    mem_out = {}

    def scores(unit):
        h, u = unit
        return _scores(km_ref[h], qm_ref[h * HEAD_DIM:(h + 1) * HEAD_DIM, u * Q_SUB:(u + 1) * Q_SUB])

    def update(unit, s3):
        _, l, acc = _softmax_update(s3, vmt_ref[unit[0]], *fresh)
        mem_out[unit] = _normalise(acc, l).astype(BF16)

    units = [(h, u) for h in range(N_MEM_HEADS) for u in range(tm // Q_SUB)]
    _run_units(units, scores, update, UNIT_LOOKAHEAD)
    mo = jnp.concatenate(
        [jnp.concatenate([mem_out[h, u] for u in range(tm // Q_SUB)], axis=1) for h in range(N_MEM_HEADS)],
        axis=0)
    y = (jnp.dot(w_ref[:, :Q_W], tok_ref[...], preferred_element_type=F32)
         + jnp.dot(w_ref[:, Q_W:], mo, preferred_element_type=F32))
    o_ref[...] = x_ref[...] + _rms_rows(y, g_ref[...])


def _out_proj(xt, tok_t, qm_t, km, vmt, wt, g_col, seq_len):
    t = xt.shape[1]
    tm = TOKEN_TILE
    per_seq = seq_len // tm
    return pl.pallas_call(
        _out_proj_kernel,
        grid=(t // tm,),
        in_specs=[
            pl.BlockSpec((D_MODEL, tm), lambda i: (0, i)),
            pl.BlockSpec((Q_W, tm), lambda i: (0, i)),
            pl.BlockSpec((QM_W, tm), lambda i: (0, i)),
            pl.BlockSpec((None, N_MEM_HEADS, N_MEM, HEAD_DIM), lambda i: (i // per_seq, 0, 0, 0)),
            pl.BlockSpec((None, N_MEM_HEADS, HEAD_DIM, N_MEM), lambda i: (i // per_seq, 0, 0, 0)),
            pl.BlockSpec((D_MODEL, MIX_WIDTH), lambda i: (0, 0)),
            pl.BlockSpec((D_MODEL, 1), lambda i: (0, 0)),
        ],
        out_specs=pl.BlockSpec((D_MODEL, tm), lambda i: (0, i)),
        out_shape=jax.ShapeDtypeStruct(xt.shape, F32),
        compiler_params=_params("parallel"),
        name="out_proj",
    )(xt, tok_t, qm_t, km, vmt, wt, g_col)


def _ffn_kernel(x_ref, gpre_ref, wg_ref, wu_ref, wd_ref, gpost_ref, o_ref, h_sc, acc_sc):
    j = pl.program_id(1)

    @pl.when(j == 0)
    def _():
        h_sc[...] = _rms_rows(x_ref[...], gpre_ref[...]).astype(BF16)
        acc_sc[...] = jnp.zeros(acc_sc.shape, F32)

    h = h_sc[...]
    gate = jnp.dot(wg_ref[...], h, preferred_element_type=F32)
    up = jnp.dot(wu_ref[...], h, preferred_element_type=F32)
    act = (gate * jax.nn.sigmoid(gate) * up).astype(BF16)
    acc_sc[...] += jnp.dot(wd_ref[...], act, preferred_element_type=F32)

    @pl.when(j == pl.num_programs(1) - 1)
    def _():
        o_ref[...] = x_ref[...] + _rms_rows(acc_sc[...], gpost_ref[...])


def _ffn(xt, gpre, wg_t, wu_t, wd_t, gpost):
    t = xt.shape[1]
    tm, tf = TOKEN_TILE, FF_TILE
    return pl.pallas_call(
        _ffn_kernel,
        grid=(t // tm, D_FF // tf),
        in_specs=[
            pl.BlockSpec((D_MODEL, tm), lambda i, j: (0, i)),
            pl.BlockSpec((D_MODEL, 1), lambda i, j: (0, 0)),
            pl.BlockSpec((tf, D_MODEL), lambda i, j: (j, 0)),
            pl.BlockSpec((tf, D_MODEL), lambda i, j: (j, 0)),
            pl.BlockSpec((D_MODEL, tf), lambda i, j: (0, j)),
            pl.BlockSpec((D_MODEL, 1), lambda i, j: (0, 0)),
        ],
        out_specs=pl.BlockSpec((D_MODEL, tm), lambda i, j: (0, i)),
        out_shape=jax.ShapeDtypeStruct(xt.shape, F32),
        scratch_shapes=[pltpu.VMEM((D_MODEL, tm), BF16), pltpu.VMEM((D_MODEL, tm), F32)],
        compiler_params=_params("parallel", "arbitrary"),
        name="ffn",
    )(xt, gpre, wg_t, wu_t, wd_t, gpost)


def _rope_table_t(pos, n_dims, theta):
    inv = theta ** (-(jnp.arange(0, n_dims, 2, dtype=F32) / n_dims))
    ang = inv[:, None] * pos.astype(F32)[None, :]
    return jnp.cos(ang), jnp.sin(ang)


def kernel(x, mem, mem_norm_g, w_in, w_mem_kv, w_o, g_mix_pre, g_mix_post, attn_sink, qk_norm_g,
           w_gate_up, w_down, g_ffn_pre, g_ffn_post):
    batch, seq_len, _ = x.shape
    assert seq_len % ATTN_TILE == 0 and seq_len % TOKEN_TILE == 0 and D_FF % FF_TILE == 0
    pos = jnp.arange(seq_len, dtype=jnp.int32)
    partial_tables = _rope_table_t(pos, ROPE_DIMS, ROPE_THETA)
    axial_tables = (_rope_table_t(pos // GRID_W, HEAD_DIM // 2, AXIAL_THETA)
                    + _rope_table_t(pos % GRID_W, HEAD_DIM // 2, AXIAL_THETA))

    w_in_t = jnp.swapaxes(w_in, 1, 2).astype(BF16)
    w_o_t = jnp.swapaxes(w_o, 1, 2).astype(BF16)
    w_gu_t = jnp.swapaxes(w_gate_up, 1, 2).astype(BF16)
    w_g_t, w_u_t = w_gu_t[:, :D_FF], w_gu_t[:, D_FF:]
    w_d_t = jnp.swapaxes(w_down, 1, 2).astype(BF16)
    w_mem = w_mem_kv.astype(BF16).reshape(DEPTH, D_MODEL, 2, N_MEM_HEADS, HEAD_DIM)
    w_mk = jnp.transpose(w_mem[:, :, 0], (0, 2, 1, 3))
    w_mv_t = jnp.transpose(w_mem[:, :, 1], (0, 2, 3, 1))

    km, vmt = _mem_kv(mem, mem_norm_g.reshape(1, D_MODEL), w_mk, w_mv_t)

    xt = _to_feature_major(x.reshape(batch * seq_len, D_MODEL))
    for i in range(DEPTH):
        kind = i % N_MIXERS
        g_pre = g_mix_pre[i].reshape(D_MODEL, 1)
        if kind == 1:
            qk_g = qk_norm_g[i // N_MIXERS].reshape(2, HEAD_DIM, 1)
            qt, kt, vt, qmt = _in_proj(xt, g_pre, w_in_t[i], axial_tables, qk_g, seq_len)
            tok_t = _full_attention(qt, kt, vt, batch, seq_len)
        else:
            qt, kt, vt, qmt = _in_proj(xt, g_pre, w_in_t[i], partial_tables, None, seq_len)
            if kind == 0:
                tok_t = _banded_attention(attn_sink[i // N_MIXERS], qt, kt, vt, batch, seq_len)
            else:
                tok_t = _dilated_attention(qt, kt, vt, batch, seq_len)
        xt = _out_proj(xt, tok_t, qmt, km[i], vmt[i], w_o_t[i], g_mix_post[i].reshape(D_MODEL, 1), seq_len)
        xt = _ffn(xt, g_ffn_pre[i].reshape(D_MODEL, 1), w_g_t[i], w_u_t[i], w_d_t[i],
                  g_ffn_post[i].reshape(D_MODEL, 1))
    return _to_token_major(xt).reshape(batch, seq_len, D_MODEL)
```

```python
import functools

import numpy as np
import jax
import jax.numpy as jnp
from jax import lax
from jax.experimental import pallas as pl
from jax.experimental.pallas import tpu as pltpu

D_MODEL = 1024
DEPTH = 4
HEAD_DIM = 64
N_TOK_HEADS = 12
N_KV_HEADS = 3
GQA_GROUP = N_TOK_HEADS // N_KV_HEADS
N_MEM_HEADS = 4
N_MEM = 256
Q_W = N_TOK_HEADS * HEAD_DIM
KV_W = N_KV_HEADS * HEAD_DIM
QM_W = N_MEM_HEADS * HEAD_DIM
IN_W = Q_W + 2 * KV_W + QM_W
MIX_WIDTH = Q_W + QM_W
D_FF = -(-8 * D_MODEL // (3 * 256)) * 256
N_MIXERS = 3
A_RADIUS = 128
C_GROUPS = ((128, 1), (512, 4), (2048, 16))
C_RADIUS = 64
ROPE_THETA = 500000.0
ROPE_DIMS = HEAD_DIM // 4
AXIAL_THETA = 10000.0
GRID_W = 64
EPS = 1e-6
ATTN_SCALE = HEAD_DIM ** -0.5
LOG2E = 1.4426950408889634
Q_SCALE = ATTN_SCALE * LOG2E
NEG_INIT = -1e30

F32 = jnp.float32
BF16 = jnp.bfloat16
BF16_ROWS = 16

V_ROWS = HEAD_DIM + BF16_ROWS
V_W = N_KV_HEADS * V_ROWS

TOKEN_TILE = 512
OUT_TILE = 1024
ATTN_TILE = 512
Q_SUB = 256
KEY_CHUNK = 256
FLASH_CHUNKS = 8
UNIT_LOOKAHEAD = 5
VMEM_LIMIT = 56 * 1024 * 1024

_NT = (((1,), (1,)), ((), ()))


def _params(*semantics):
    return pltpu.CompilerParams(dimension_semantics=semantics, vmem_limit_bytes=VMEM_LIMIT)


def _rms_rows(x, g_col):
    ms = jnp.mean(x * x, axis=0, keepdims=True)
    return x * lax.rsqrt(ms + EPS) * g_col


def _ones_tile(n):
    return jnp.where(lax.broadcasted_iota(jnp.int32, (BF16_ROWS, n), 0) == 0, 1.0, 0.0).astype(BF16)


def _mem_kv_kernel(mem_ref, g_ref, wk_ref, wvt_ref, km_ref, vmt_ref):
    m = mem_ref[...]
    ms = jnp.mean(m * m, axis=-1, keepdims=True)
    mn = (m * lax.rsqrt(ms + EPS) * g_ref[...]).astype(BF16)
    for h in range(N_MEM_HEADS):
        km_ref[h] = jnp.dot(mn, wk_ref[h], preferred_element_type=F32).astype(BF16)
        vmt_ref[h, :HEAD_DIM] = lax.dot_general(wvt_ref[h], mn, _NT, preferred_element_type=F32).astype(BF16)
        vmt_ref[h, HEAD_DIM:] = _ones_tile(N_MEM)


def _mem_kv(mem, g_row, wk, wvt):
    b = mem.shape[0]
    return pl.pallas_call(
        _mem_kv_kernel,
        grid=(DEPTH, b),
        in_specs=[
            pl.BlockSpec((None, N_MEM, D_MODEL), lambda l, i: (i, 0, 0)),
            pl.BlockSpec((1, D_MODEL), lambda l, i: (0, 0)),
            pl.BlockSpec((None, N_MEM_HEADS, D_MODEL, HEAD_DIM), lambda l, i: (l, 0, 0, 0)),
            pl.BlockSpec((None, N_MEM_HEADS, HEAD_DIM, D_MODEL), lambda l, i: (l, 0, 0, 0)),
        ],
        out_specs=[
            pl.BlockSpec((None, None, N_MEM_HEADS, N_MEM, HEAD_DIM), lambda l, i: (l, i, 0, 0, 0)),
            pl.BlockSpec((None, None, N_MEM_HEADS, V_ROWS, N_MEM), lambda l, i: (l, i, 0, 0, 0)),
        ],
        out_shape=[
            jax.ShapeDtypeStruct((DEPTH, b, N_MEM_HEADS, N_MEM, HEAD_DIM), BF16),
            jax.ShapeDtypeStruct((DEPTH, b, N_MEM_HEADS, V_ROWS, N_MEM), BF16),
        ],
        compiler_params=_params("parallel", "parallel"),
        name="mem_kv",
    )(mem, g_row, wk, wvt)


def _split_store(p, rot, q_ref, k_ref, v_ref, qm_ref):
    for hd in range(N_TOK_HEADS):
        r = hd * HEAD_DIM
        q_ref[r:r + HEAD_DIM, :] = (rot(p[r:r + HEAD_DIM], False) * Q_SCALE).astype(BF16)
    ones = _ones_tile(p.shape[1])
    for hd in range(N_KV_HEADS):
        r = hd * HEAD_DIM
        k_ref[r:r + HEAD_DIM, :] = rot(p[Q_W + r:Q_W + r + HEAD_DIM], True).astype(BF16)
        v_ref[hd * V_ROWS:hd * V_ROWS + HEAD_DIM, :] = p[Q_W + KV_W + r:Q_W + KV_W + r + HEAD_DIM].astype(BF16)
        v_ref[hd * V_ROWS + HEAD_DIM:(hd + 1) * V_ROWS, :] = ones
    qm_ref[...] = (p[Q_W + 2 * KV_W:] * Q_SCALE).astype(BF16)


def _partial_rope_store(p, cos_ref, sin_ref, q_ref, k_ref, v_ref, qm_ref):
    c, s = cos_ref[...], sin_ref[...]
    half = ROPE_DIMS // 2

    def rot(head, is_key):
        x1, x2 = head[:half], head[half:ROPE_DIMS]
        return jnp.concatenate([x1 * c - x2 * s, x2 * c + x1 * s, head[ROPE_DIMS:]], axis=0)

    _split_store(p, rot, q_ref, k_ref, v_ref, qm_ref)


def _in_proj_partial_kernel(x_ref, g_ref, w_ref, cos_ref, sin_ref, q_ref, k_ref, v_ref, qm_ref):
    h = _rms_rows(x_ref[...], g_ref[...]).astype(BF16)
    p = jnp.dot(w_ref[...], h, preferred_element_type=F32)
    _partial_rope_store(p, cos_ref, sin_ref, q_ref, k_ref, v_ref, qm_ref)


def _in_proj_entry_kernel(x_ref, g_ref, w_ref, cos_ref, sin_ref, xt_ref, q_ref, k_ref, v_ref, qm_ref):
    x = x_ref[...]
    ms = jnp.mean(x * x, axis=-1, keepdims=True)
    h = (x * lax.rsqrt(ms + EPS) * g_ref[...]).astype(BF16)
    p = lax.dot_general(w_ref[...], h, _NT, preferred_element_type=F32)
    xt_ref[...] = x.T
    _partial_rope_store(p, cos_ref, sin_ref, q_ref, k_ref, v_ref, qm_ref)


def _in_proj_axial_kernel(x_ref, g_ref, w_ref, cr_ref, sr_ref, cc_ref, sc_ref, qkg_ref,
                          q_ref, k_ref, v_ref, qm_ref):
    h = _rms_rows(x_ref[...], g_ref[...]).astype(BF16)
    p = jnp.dot(w_ref[...], h, preferred_element_type=F32)
    cr, sr, cc, sc = cr_ref[...], sr_ref[...], cc_ref[...], sc_ref[...]
    qr = HEAD_DIM // 4

    def rot(head, is_key):
        y = _rms_rows(head, qkg_ref[1] if is_key else qkg_ref[0])
        a1, a2, b1, b2 = y[:qr], y[qr:2 * qr], y[2 * qr:3 * qr], y[3 * qr:]
        return jnp.concatenate(
            [a1 * cr - a2 * sr, a2 * cr + a1 * sr, b1 * cc - b2 * sc, b2 * cc + b1 * sc], axis=0)

    _split_store(p, rot, q_ref, k_ref, v_ref, qm_ref)


def _in_proj(x, g, wt, tables, qk_g, seq_len, entry=False):
    tm = TOKEN_TILE
    t = x.shape[0] if entry else x.shape[1]
    n_seq_tiles = seq_len // tm
    tab_specs = [pl.BlockSpec((tb.shape[0], tm), lambda i: (0, i % n_seq_tiles)) for tb in tables]
    x_spec = pl.BlockSpec((tm, D_MODEL), lambda i: (i, 0)) if entry else pl.BlockSpec((D_MODEL, tm), lambda i: (0, i))
    in_specs = [x_spec, pl.BlockSpec(g.shape, lambda i: (0, 0)), pl.BlockSpec((IN_W, D_MODEL), lambda i: (0, 0))]
    in_specs += tab_specs
    args = [x, g, wt] + list(tables)
    widths, dtypes = (Q_W, KV_W, V_W, QM_W), (BF16,) * 4
    if entry:
        assert qk_g is None
        body = _in_proj_entry_kernel
        widths, dtypes = (D_MODEL,) + widths, (F32,) + dtypes
    elif qk_g is None:
        body = _in_proj_partial_kernel
    else:
        body = _in_proj_axial_kernel
        in_specs.append(pl.BlockSpec((2, HEAD_DIM, 1), lambda i: (0, 0, 0)))
        args.append(qk_g)
    return pl.pallas_call(
        body,
        grid=(t // tm,),
        in_specs=in_specs,
        out_specs=[pl.BlockSpec((w, tm), lambda i: (0, i)) for w in widths],
        out_shape=[jax.ShapeDtypeStruct((w, t), dt) for w, dt in zip(widths, dtypes)],
        compiler_params=_params("parallel"),
        name="in_proj",
    )(*args)


def _sublane_all(x, op):
    for shift in (4, 2, 1):
        x = op(x, pltpu.roll(x, shift, 0))
    return x


def _scores(kc, q, bias=None):
    s = jnp.dot(kc, q, preferred_element_type=F32)
    if bias is not None:
        s = s + bias
    return s.reshape(kc.shape[0] // 8, 8, q.shape[1])


def _softmax_update(s3, vc, m_old, l_old, acc_old):
    n, _, qs = s3.shape
    m_new = jnp.maximum(m_old, _sublane_all(jnp.max(s3, axis=0), jnp.maximum))
    alpha = jnp.exp2(m_old - m_new)
    p = jnp.exp2((s3 - m_new[None]).reshape(n * 8, qs).astype(BF16))
    pv = jnp.dot(vc, p, preferred_element_type=F32)
    l_new = alpha * l_old + pv[HEAD_DIM:HEAD_DIM + 8]
    acc3 = alpha[None] * acc_old.reshape(HEAD_DIM // 8, 8, qs) + pv[:HEAD_DIM].reshape(HEAD_DIM // 8, 8, qs)
    return m_new, l_new, acc3.reshape(HEAD_DIM, qs)


def _fresh_state():
    return (jnp.full((8, Q_SUB), NEG_INIT, F32), jnp.zeros((8, Q_SUB), F32),
            jnp.zeros((HEAD_DIM, Q_SUB), F32))


def _run_units(units, score_fn, update_fn, lookahead):
    pending = {}
    for n in range(len(units) + lookahead):
        if n < len(units):
            pending[n] = score_fn(units[n])
        if n >= lookahead:
            update_fn(units[n - lookahead], pending.pop(n - lookahead))


def _scale_rows(acc, w):
    qs = acc.shape[1]
    return (acc.reshape(HEAD_DIM // 8, 8, qs) * w[None]).reshape(HEAD_DIM, qs)


def _normalise(acc, l):
    return _scale_rows(acc, 1.0 / _sublane_all(l, jnp.add))


def _band_bias(delta0, reach, dil, row_lo=0, row_hi=KEY_CHUNK):
    kr = np.arange(KEY_CHUNK)[:, None]
    delta = delta0 + kr - np.arange(Q_SUB)[None, :]
    ok = (np.abs(delta) <= reach) & (delta % dil == 0) & (kr >= row_lo) & (kr < row_hi)
    return np.where(ok, 0.0, -np.inf).astype(np.float32)


def _window_chunks(reach):
    pad = -(-reach // 128) * 128
    return pad, (Q_SUB + 2 * pad) // KEY_CHUNK


def _window_biases(reach, dil):
    pad, n_chunk = _window_chunks(reach)
    out = [_band_bias(c * KEY_CHUNK - pad, reach, dil) for c in range(n_chunk)]
    if pad % KEY_CHUNK:
        out.append(_band_bias(-pad, reach, dil, row_lo=pad % KEY_CHUNK))
        out.append(_band_bias((n_chunk - 1) * KEY_CHUNK - pad, reach, dil, row_hi=KEY_CHUNK - pad % KEY_CHUNK))
    return out


def _bias_index(start, c, pad, n_chunk, base, masked_index, seq_len):
    if pad % KEY_CHUNK == 0:
        on = (start >= 0) & (start + KEY_CHUNK <= seq_len)
        return jnp.where(on, base + c, masked_index)
    idx = base + c
    if c == 0:
        idx = jnp.where(start < 0, base + n_chunk, idx)
    if c == n_chunk - 1:
        idx = jnp.where(start + KEY_CHUNK > seq_len, base + n_chunk + 1, idx)
    return idx


def _banded_kernel(sink_ref, bias_ref, q_ref, kl_ref, km_ref, kr_ref, vl_ref, vm_ref, vr_ref, o_ref, *,
                   seq_len):
    g = pl.program_id(1)
    q0 = pl.program_id(2) * ATTN_TILE
    pad, n_chunk = _window_chunks(A_RADIUS)
    kwin = jnp.concatenate([kl_ref[...], km_ref[...], kr_ref[...]], axis=1).T
    vwin = jnp.concatenate([vl_ref[...], vm_ref[...], vr_ref[...]], axis=1)
    n_sub = ATTN_TILE // Q_SUB

    first_sublane = lax.broadcasted_iota(jnp.int32, (8, Q_SUB), 0) == 0
    state = {}
    for j in range(GQA_GROUP):
        sink = sink_ref[g * GQA_GROUP + j] * LOG2E
        for u in range(n_sub):
            state[j, u] = (jnp.full((8, Q_SUB), sink, F32), jnp.where(first_sublane, 1.0, 0.0),
                           jnp.zeros((HEAD_DIM, Q_SUB), F32))

    def scores(unit):
        u, c, j = unit
        w0 = u * Q_SUB + c * KEY_CHUNK
        q = q_ref[j * HEAD_DIM:(j + 1) * HEAD_DIM, u * Q_SUB:(u + 1) * Q_SUB]
        idx = _bias_index(q0 + w0 - pad, c, pad, n_chunk, 0, None, seq_len)
        return _scores(kwin[w0:w0 + KEY_CHUNK], q, bias_ref[idx])

    def update(unit, s3):
        u, c, j = unit
        w0 = u * Q_SUB + c * KEY_CHUNK
        state[j, u] = _softmax_update(s3, vwin[:, w0:w0 + KEY_CHUNK], *state[j, u])

    units = [(u, c, j) for u in range(n_sub) for c in range(n_chunk) for j in range(GQA_GROUP)]
    _run_units(units, scores, update, UNIT_LOOKAHEAD)

    for (j, u), (_, l, acc) in state.items():
        o_ref[j * HEAD_DIM:(j + 1) * HEAD_DIM, u * Q_SUB:(u + 1) * Q_SUB] = _normalise(acc, l).astype(BF16)


def _banded_attention(sink, qt, kt, vt, batch, seq_len):
    tq = ATTN_TILE
    nq = seq_len // tq
    pad, _ = _window_chunks(A_RADIUS)
    halo_per_tile = tq // pad
    n_halo = seq_len // pad
    bias = jnp.asarray(np.stack(_window_biases(A_RADIUS, 1)))

    def left(b, g, i):
        return (g, b * n_halo + jnp.maximum(i * halo_per_tile - 1, 0))

    def right(b, g, i):
        return (g, b * n_halo + jnp.minimum((i + 1) * halo_per_tile, n_halo - 1))

    def main(b, g, i):
        return (g, b * nq + i)

    def kv_specs(rows):
        return [pl.BlockSpec((rows, pad), left), pl.BlockSpec((rows, tq), main), pl.BlockSpec((rows, pad), right)]

    return pl.pallas_call(
        functools.partial(_banded_kernel, seq_len=seq_len),
        grid=(batch, N_KV_HEADS, nq),
        in_specs=[
            pl.BlockSpec(memory_space=pltpu.SMEM),
            pl.BlockSpec(bias.shape, lambda b, g, i: (0, 0, 0)),
            pl.BlockSpec((GQA_GROUP * HEAD_DIM, tq), main),
        ] + kv_specs(HEAD_DIM) + kv_specs(V_ROWS),
        out_specs=pl.BlockSpec((GQA_GROUP * HEAD_DIM, tq), main),
        out_shape=jax.ShapeDtypeStruct(qt.shape, BF16),
        compiler_params=_params("parallel", "parallel", "parallel"),
        name="banded_attention",
    )(sink, bias, qt, kt, kt, kt, vt, vt, vt)


def _flash_kernel(q_ref, k_ref, v_ref, o_ref, k_sc, m_sc, l_sc, acc_sc, *, seq_len):
    tq = q_ref.shape[1]

    @pl.when(pl.program_id(2) == 0)
    def _():
        def fill(c, carry):
            off = pl.multiple_of(c * ATTN_TILE, ATTN_TILE)
            k_sc[pl.ds(off, ATTN_TILE), :] = k_ref[:, pl.ds(off, ATTN_TILE)].T
            return carry
        lax.fori_loop(0, seq_len // ATTN_TILE, fill, 0)

    m_sc[...] = jnp.full(m_sc.shape, NEG_INIT, F32)
    l_sc[...] = jnp.zeros(l_sc.shape, F32)
    acc_sc[...] = jnp.zeros(acc_sc.shape, F32)

    subs = [(j, u) for j in range(GQA_GROUP) for u in range(tq // Q_SUB)]
    units = [(c, j, u) for c in range(FLASH_CHUNKS) for (j, u) in subs]

    def body(it, carry):
        kcs, vcs = [], []
        for c in range(FLASH_CHUNKS):
            off = pl.multiple_of((it * FLASH_CHUNKS + c) * KEY_CHUNK, KEY_CHUNK)
            kcs.append(k_sc[pl.ds(off, KEY_CHUNK), :])
            vcs.append(v_ref[:, pl.ds(off, KEY_CHUNK)])
        state = {}
        for j, u in subs:
            cols = slice(u * Q_SUB, (u + 1) * Q_SUB)
            state[j, u] = (m_sc[j, :, cols], l_sc[j, :, cols], acc_sc[j, :, cols])

        def scores(unit):
            c, j, u = unit
            return _scores(kcs[c], q_ref[j * HEAD_DIM:(j + 1) * HEAD_DIM, u * Q_SUB:(u + 1) * Q_SUB])

        def update(unit, s3):
            c, j, u = unit
            state[j, u] = _softmax_update(s3, vcs[c], *state[j, u])

        _run_units(units, scores, update, UNIT_LOOKAHEAD)
        for j, u in subs:
            cols = slice(u * Q_SUB, (u + 1) * Q_SUB)
            m_sc[j, :, cols], l_sc[j, :, cols], acc_sc[j, :, cols] = state[j, u]
        return carry

    lax.fori_loop(0, seq_len // (KEY_CHUNK * FLASH_CHUNKS), body, 0)

    for j in range(GQA_GROUP):
        o_ref[j * HEAD_DIM:(j + 1) * HEAD_DIM, :] = _normalise(acc_sc[j], l_sc[j]).astype(BF16)


def _full_attention(qt, kt, vt, batch, seq_len):
    tq = ATTN_TILE
    nq = seq_len // tq
    return pl.pallas_call(
        functools.partial(_flash_kernel, seq_len=seq_len),
        grid=(batch, N_KV_HEADS, nq),
        in_specs=[
            pl.BlockSpec((GQA_GROUP * HEAD_DIM, tq), lambda b, g, i: (g, b * nq + i)),
            pl.BlockSpec((HEAD_DIM, seq_len), lambda b, g, i: (g, b)),
            pl.BlockSpec((V_ROWS, seq_len), lambda b, g, i: (g, b)),
        ],
        out_specs=pl.BlockSpec((GQA_GROUP * HEAD_DIM, tq), lambda b, g, i: (g, b * nq + i)),
        out_shape=jax.ShapeDtypeStruct(qt.shape, BF16),
        scratch_shapes=[
            pltpu.VMEM((seq_len, HEAD_DIM), BF16),
            pltpu.VMEM((GQA_GROUP, 8, tq), F32),
            pltpu.VMEM((GQA_GROUP, 8, tq), F32),
            pltpu.VMEM((GQA_GROUP, HEAD_DIM, tq), F32),
        ],
        compiler_params=_params("parallel", "parallel", "arbitrary"),
        name="full_attention",
    )(qt, kt, vt)


C_BLOCKS = 5


def _dilated_bias_table():
    blocks, bases = [], []
    for _, dil in C_GROUPS:
        bases.append(len(blocks))
        blocks += _window_biases(C_RADIUS * dil, dil)
    blocks.append(np.full((KEY_CHUNK, Q_SUB), -np.inf, np.float32))
    return np.stack(blocks), bases


def _dilated_kernel(bias_ref, q_ref, *refs, seq_len, bases):
    k_refs, v_refs, o_ref = refs[:C_BLOCKS], refs[C_BLOCKS:2 * C_BLOCKS], refs[2 * C_BLOCKS]
    q0 = pl.program_id(1) * ATTN_TILE
    centre = (C_BLOCKS // 2) * ATTN_TILE
    kwin = jnp.concatenate([r[...] for r in k_refs], axis=1)
    vwin = jnp.concatenate([r[...] for r in v_refs], axis=1)
    masked_index = bias_ref.shape[0] - 1
    n_sub = ATTN_TILE // Q_SUB

    geom, keys, units, state = {}, {}, [], {}
    for g, (_, dil) in enumerate(C_GROUPS):
        pad, n_chunk = _window_chunks(C_RADIUS * dil)
        geom[g] = (pad, n_chunk)
        assert centre - pad >= 0 and centre + ATTN_TILE + pad <= C_BLOCKS * ATTN_TILE
        keys[g] = kwin[g * HEAD_DIM:(g + 1) * HEAD_DIM, centre - pad:centre + ATTN_TILE + pad].T
        units += [(g, u, c, j) for u in range(n_sub) for c in range(n_chunk) for j in range(GQA_GROUP)]
        for j in range(GQA_GROUP):
            for u in range(n_sub):
                state[g, j, u] = _fresh_state()

    def scores(unit):
        g, u, c, j = unit
        pad, n_chunk = geom[g]
        w0 = u * Q_SUB + c * KEY_CHUNK
        head = g * GQA_GROUP + j
        q = q_ref[head * HEAD_DIM:(head + 1) * HEAD_DIM, u * Q_SUB:(u + 1) * Q_SUB]
        idx = _bias_index(q0 + w0 - pad, c, pad, n_chunk, bases[g], masked_index, seq_len)
        return _scores(keys[g][w0:w0 + KEY_CHUNK], q, bias_ref[idx])

    def update(unit, s3):
        g, u, c, j = unit
        col = centre - geom[g][0] + u * Q_SUB + c * KEY_CHUNK
        vc = vwin[g * V_ROWS:(g + 1) * V_ROWS, col:col + KEY_CHUNK]
        state[g, j, u] = _softmax_update(s3, vc, *state[g, j, u])

    _run_units(units, scores, update, UNIT_LOOKAHEAD)

    groups = range(len(C_GROUPS))
    for j in range(GQA_GROUP):
        for u in range(n_sub):
            mx = functools.reduce(jnp.maximum, [state[g, j, u][0] for g in groups])
            es = [jnp.exp2(state[g, j, u][0] - mx) for g in groups]
            tot = functools.reduce(
                lambda a, b: a + b, [_sublane_all(state[g, j, u][1], jnp.add) * es[g] for g in groups])
            for g in groups:
                r = (g * GQA_GROUP + j) * HEAD_DIM
                o_ref[r:r + HEAD_DIM, u * Q_SUB:(u + 1) * Q_SUB] = _scale_rows(
                    state[g, j, u][2], es[g] / tot).astype(BF16)


def _dilated_attention(qt, kt, vt, batch, seq_len):
    tq = ATTN_TILE
    nq = seq_len // tq
    table, bases = _dilated_bias_table()
    bias = jnp.asarray(table)

    def block(rows, off):
        return pl.BlockSpec((rows, tq), lambda b, i: (0, b * nq + jnp.clip(i + off, 0, nq - 1)))

    offs = range(-(C_BLOCKS // 2), C_BLOCKS // 2 + 1)
    return pl.pallas_call(
        functools.partial(_dilated_kernel, seq_len=seq_len, bases=tuple(bases)),
        grid=(batch, nq),
        in_specs=[pl.BlockSpec(bias.shape, lambda b, i: (0, 0, 0)),
                  pl.BlockSpec((Q_W, tq), lambda b, i: (0, b * nq + i))]
        + [block(KV_W, o) for o in offs] + [block(V_W, o) for o in offs],
        out_specs=pl.BlockSpec((Q_W, tq), lambda b, i: (0, b * nq + i)),
        out_shape=jax.ShapeDtypeStruct(qt.shape, BF16),
        compiler_params=_params("parallel", "parallel"),
        name="dilated_attention",
    )(bias, qt, *([kt] * C_BLOCKS), *([vt] * C_BLOCKS))


def _out_proj_kernel(x_ref, tok_ref, qm_ref, km_ref, vmt_ref, w_ref, g_ref, o_ref):
    tm = x_ref.shape[1]
    mem_out = {}

    def scores(unit):
        h, u = unit
        return _scores(km_ref[h], qm_ref[h * HEAD_DIM:(h + 1) * HEAD_DIM, u * Q_SUB:(u + 1) * Q_SUB])

    def update(unit, s3):
        _, l, acc = _softmax_update(s3, vmt_ref[unit[0]], *_fresh_state())
        mem_out[unit] = _normalise(acc, l).astype(BF16)

    units = [(h, u) for h in range(N_MEM_HEADS) for u in range(tm // Q_SUB)]
    _run_units(units, scores, update, UNIT_LOOKAHEAD)
    mo = jnp.concatenate(
        [jnp.concatenate([mem_out[h, u] for u in range(tm // Q_SUB)], axis=1) for h in range(N_MEM_HEADS)],
        axis=0)
    y = (jnp.dot(w_ref[:, :Q_W], tok_ref[...], preferred_element_type=F32)
         + jnp.dot(w_ref[:, Q_W:], mo, preferred_element_type=F32))
    o_ref[...] = x_ref[...] + _rms_rows(y, g_ref[...])


def _out_proj(xt, tok_t, qm_t, km, vmt, wt, g_col, seq_len):
    t = xt.shape[1]
    tm = OUT_TILE
    per_seq = seq_len // tm
    return pl.pallas_call(
        _out_proj_kernel,
        grid=(t // tm,),
        in_specs=[
            pl.BlockSpec((D_MODEL, tm), lambda i: (0, i)),
            pl.BlockSpec((Q_W, tm), lambda i: (0, i)),
            pl.BlockSpec((QM_W, tm), lambda i: (0, i)),
            pl.BlockSpec((None, N_MEM_HEADS, N_MEM, HEAD_DIM), lambda i: (i // per_seq, 0, 0, 0)),
            pl.BlockSpec((None, N_MEM_HEADS, V_ROWS, N_MEM), lambda i: (i // per_seq, 0, 0, 0)),
            pl.BlockSpec((D_MODEL, MIX_WIDTH), lambda i: (0, 0)),
            pl.BlockSpec((D_MODEL, 1), lambda i: (0, 0)),
        ],
        out_specs=pl.BlockSpec((D_MODEL, tm), lambda i: (0, i)),
        out_shape=jax.ShapeDtypeStruct(xt.shape, F32),
        compiler_params=_params("parallel"),
        name="out_proj",
    )(xt, tok_t, qm_t, km, vmt, wt, g_col)


def _ffn_kernel(x_ref, gpre_ref, wgu_ref, wd_ref, gpost_ref, o_ref, *, token_major_out):
    h = _rms_rows(x_ref[...], gpre_ref[...]).astype(BF16)
    gu = jnp.dot(wgu_ref[...], h, preferred_element_type=F32)
    gate, up = gu[:D_FF], gu[D_FF:]
    act = (gate * jax.nn.sigmoid(gate) * up).astype(BF16)
    y = jnp.dot(wd_ref[...], act, preferred_element_type=F32)
    out = x_ref[...] + _rms_rows(y, gpost_ref[...])
    o_ref[...] = out.T if token_major_out else out


def _ffn(xt, gpre, wgu_t, wd_t, gpost, token_major_out=False):
    d, t = xt.shape
    tm = TOKEN_TILE
    resident = pl.Buffered(1)
    if token_major_out:
        out_spec, out_shape = pl.BlockSpec((tm, d), lambda i: (i, 0)), (t, d)
    else:
        out_spec, out_shape = pl.BlockSpec((d, tm), lambda i: (0, i)), (d, t)
    return pl.pallas_call(
        functools.partial(_ffn_kernel, token_major_out=token_major_out),
        grid=(t // tm,),
        in_specs=[
            pl.BlockSpec((d, tm), lambda i: (0, i)),
            pl.BlockSpec((d, 1), lambda i: (0, 0)),
            pl.BlockSpec((2 * D_FF, d), lambda i: (0, 0), pipeline_mode=resident),
            pl.BlockSpec((d, D_FF), lambda i: (0, 0), pipeline_mode=resident),
            pl.BlockSpec((d, 1), lambda i: (0, 0)),
        ],
        out_specs=out_spec,
        out_shape=jax.ShapeDtypeStruct(out_shape, F32),
        compiler_params=_params("parallel"),
        name="ffn",
    )(xt, gpre, wgu_t, wd_t, gpost)


def _rope_table_t(pos, n_dims, theta):
    inv = theta ** (-(jnp.arange(0, n_dims, 2, dtype=F32) / n_dims))
    ang = inv[:, None] * pos.astype(F32)[None, :]
    return jnp.cos(ang), jnp.sin(ang)


def kernel(x, mem, mem_norm_g, w_in, w_mem_kv, w_o, g_mix_pre, g_mix_post, attn_sink, qk_norm_g,
           w_gate_up, w_down, g_ffn_pre, g_ffn_post):
    batch, seq_len, _ = x.shape
    assert seq_len % (KEY_CHUNK * FLASH_CHUNKS) == 0 and seq_len % OUT_TILE == 0
    pos = jnp.arange(seq_len, dtype=jnp.int32)
    partial_tables = _rope_table_t(pos, ROPE_DIMS, ROPE_THETA)
    axial_tables = (_rope_table_t(pos // GRID_W, HEAD_DIM // 2, AXIAL_THETA)
                    + _rope_table_t(pos % GRID_W, HEAD_DIM // 2, AXIAL_THETA))

    w_in_t = jnp.swapaxes(w_in, 1, 2).astype(BF16)
    w_o_t = jnp.swapaxes(w_o, 1, 2).astype(BF16)
    w_gu_t = jnp.swapaxes(w_gate_up, 1, 2).astype(BF16)
    w_d_t = jnp.swapaxes(w_down, 1, 2).astype(BF16)
    w_mem = w_mem_kv.astype(BF16).reshape(DEPTH, D_MODEL, 2, N_MEM_HEADS, HEAD_DIM)
    w_mk = jnp.transpose(w_mem[:, :, 0], (0, 2, 1, 3))
    w_mv_t = jnp.transpose(w_mem[:, :, 1], (0, 2, 3, 1))

    km, vmt = _mem_kv(mem, mem_norm_g.reshape(1, D_MODEL), w_mk, w_mv_t)

    xt = None
    for i in range(DEPTH):
        kind = i % N_MIXERS
        g_pre = g_mix_pre[i].reshape(D_MODEL, 1)
        if kind == 1:
            qk_g = qk_norm_g[i // N_MIXERS].reshape(2, HEAD_DIM, 1)
            qt, kt, vt, qmt = _in_proj(xt, g_pre, w_in_t[i], axial_tables, qk_g, seq_len)
            tok_t = _full_attention(qt, kt, vt, batch, seq_len)
        else:
            if i == 0:
                xt, qt, kt, vt, qmt = _in_proj(x.reshape(batch * seq_len, D_MODEL), g_pre.reshape(1, D_MODEL),
                                               w_in_t[i], partial_tables, None, seq_len, entry=True)
            else:
                qt, kt, vt, qmt = _in_proj(xt, g_pre, w_in_t[i], partial_tables, None, seq_len)
            if kind == 0:
                tok_t = _banded_attention(attn_sink[i // N_MIXERS], qt, kt, vt, batch, seq_len)
            else:
                tok_t = _dilated_attention(qt, kt, vt, batch, seq_len)
        xt = _out_proj(xt, tok_t, qmt, km[i], vmt[i], w_o_t[i], g_mix_post[i].reshape(D_MODEL, 1), seq_len)
        xt = _ffn(xt, g_ffn_pre[i].reshape(D_MODEL, 1), w_gu_t[i], w_d_t[i], g_ffn_post[i].reshape(D_MODEL, 1),
                  token_major_out=(i == DEPTH - 1))
    return xt.reshape(batch, seq_len, D_MODEL)
```

```python
import functools

import numpy as np
import jax
import jax.numpy as jnp
from jax import lax
from jax.experimental import pallas as pl
from jax.experimental.pallas import tpu as pltpu

D_MODEL = 1024
DEPTH = 4
HEAD_DIM = 64
N_TOK_HEADS = 12
N_KV_HEADS = 3
GQA_GROUP = N_TOK_HEADS // N_KV_HEADS
N_MEM_HEADS = 4
N_MEM = 256
Q_W = N_TOK_HEADS * HEAD_DIM
KV_W = N_KV_HEADS * HEAD_DIM
QM_W = N_MEM_HEADS * HEAD_DIM
IN_W = Q_W + 2 * KV_W + QM_W
MIX_WIDTH = Q_W + QM_W
D_FF = -(-8 * D_MODEL // (3 * 256)) * 256
N_MIXERS = 3
A_RADIUS = 128
C_GROUPS = ((128, 1), (512, 4), (2048, 16))
C_RADIUS = 64
ROPE_THETA = 500000.0
ROPE_DIMS = HEAD_DIM // 4
AXIAL_THETA = 10000.0
GRID_W = 64
EPS = 1e-6
ATTN_SCALE = HEAD_DIM ** -0.5
LOG2E = 1.4426950408889634
Q_SCALE = ATTN_SCALE * LOG2E
NEG_INIT = -1e30

F32 = jnp.float32
BF16 = jnp.bfloat16
BF16_ROWS = 16

V_ROWS = HEAD_DIM + BF16_ROWS
V_W = N_KV_HEADS * V_ROWS

TOKEN_TILE = 512
OUT_TILE = 1024
OUT_ROWS = 256
ATTN_TILE = 512
Q_SUB = 256
KEY_CHUNK = 256
FLASH_CHUNKS = 16
UNIT_LOOKAHEAD = 5
VMEM_LIMIT = 56 * 1024 * 1024

_NT = (((1,), (1,)), ((), ()))


def _params(*semantics):
    return pltpu.CompilerParams(dimension_semantics=semantics, vmem_limit_bytes=VMEM_LIMIT)


def _rms_rows(x, g_col):
    ms = jnp.mean(x * x, axis=0, keepdims=True)
    return x * lax.rsqrt(ms + EPS) * g_col


def _ones_tile(n):
    return jnp.where(lax.broadcasted_iota(jnp.int32, (BF16_ROWS, n), 0) == 0, 1.0, 0.0).astype(BF16)


def _mem_kv_kernel(mem_ref, g_ref, wk_ref, wvt_ref, km_ref, vmt_ref):
    m = mem_ref[...]
    ms = jnp.mean(m * m, axis=-1, keepdims=True)
    mn = (m * lax.rsqrt(ms + EPS) * g_ref[...]).astype(BF16)
    for h in range(N_MEM_HEADS):
        km_ref[h] = jnp.dot(mn, wk_ref[h], preferred_element_type=F32).astype(BF16)
        vmt_ref[h, :HEAD_DIM] = lax.dot_general(wvt_ref[h], mn, _NT, preferred_element_type=F32).astype(BF16)
        vmt_ref[h, HEAD_DIM:] = _ones_tile(N_MEM)


def _mem_kv(mem, g_row, wk, wvt):
    b = mem.shape[0]
    return pl.pallas_call(
        _mem_kv_kernel,
        grid=(DEPTH, b),
        in_specs=[
            pl.BlockSpec((None, N_MEM, D_MODEL), lambda l, i: (i, 0, 0)),
            pl.BlockSpec((1, D_MODEL), lambda l, i: (0, 0)),
            pl.BlockSpec((None, N_MEM_HEADS, D_MODEL, HEAD_DIM), lambda l, i: (l, 0, 0, 0)),
            pl.BlockSpec((None, N_MEM_HEADS, HEAD_DIM, D_MODEL), lambda l, i: (l, 0, 0, 0)),
        ],
        out_specs=[
            pl.BlockSpec((None, None, N_MEM_HEADS, N_MEM, HEAD_DIM), lambda l, i: (l, i, 0, 0, 0)),
            pl.BlockSpec((None, None, N_MEM_HEADS, V_ROWS, N_MEM), lambda l, i: (l, i, 0, 0, 0)),
        ],
        out_shape=[
            jax.ShapeDtypeStruct((DEPTH, b, N_MEM_HEADS, N_MEM, HEAD_DIM), BF16),
            jax.ShapeDtypeStruct((DEPTH, b, N_MEM_HEADS, V_ROWS, N_MEM), BF16),
        ],
        compiler_params=_params("parallel", "parallel"),
        name="mem_kv",
    )(mem, g_row, wk, wvt)


def _split_store(p, rot, q_ref, k_ref, v_ref, qm_ref):
    for hd in range(N_TOK_HEADS):
        r = hd * HEAD_DIM
        q_ref[r:r + HEAD_DIM, :] = (rot(p[r:r + HEAD_DIM], False) * Q_SCALE).astype(BF16)
    ones = _ones_tile(p.shape[1])
    for hd in range(N_KV_HEADS):
        r = hd * HEAD_DIM
        k_ref[r:r + HEAD_DIM, :] = rot(p[Q_W + r:Q_W + r + HEAD_DIM], True).astype(BF16)
        v_ref[hd * V_ROWS:hd * V_ROWS + HEAD_DIM, :] = p[Q_W + KV_W + r:Q_W + KV_W + r + HEAD_DIM].astype(BF16)
        v_ref[hd * V_ROWS + HEAD_DIM:(hd + 1) * V_ROWS, :] = ones
    qm_ref[...] = (p[Q_W + 2 * KV_W:] * Q_SCALE).astype(BF16)


def _partial_rope_store(p, cos_ref, sin_ref, q_ref, k_ref, v_ref, qm_ref):
    c, s = cos_ref[...], sin_ref[...]
    half = ROPE_DIMS // 2

    def rot(head, is_key):
        x1, x2 = head[:half], head[half:ROPE_DIMS]
        return jnp.concatenate([x1 * c - x2 * s, x2 * c + x1 * s, head[ROPE_DIMS:]], axis=0)

    _split_store(p, rot, q_ref, k_ref, v_ref, qm_ref)


def _in_proj_partial_kernel(x_ref, g_ref, w_ref, cos_ref, sin_ref, q_ref, k_ref, v_ref, qm_ref):
    h = _rms_rows(x_ref[...], g_ref[...]).astype(BF16)
    p = jnp.dot(w_ref[...], h, preferred_element_type=F32)
    _partial_rope_store(p, cos_ref, sin_ref, q_ref, k_ref, v_ref, qm_ref)


def _in_proj_entry_kernel(x_ref, g_ref, w_ref, cos_ref, sin_ref, xt_ref, q_ref, k_ref, v_ref, qm_ref):
    x = x_ref[...]
    ms = jnp.mean(x * x, axis=-1, keepdims=True)
    h = (x * lax.rsqrt(ms + EPS) * g_ref[...]).astype(BF16)
    p = lax.dot_general(w_ref[...], h, _NT, preferred_element_type=F32)
    xt_ref[...] = x.T
    _partial_rope_store(p, cos_ref, sin_ref, q_ref, k_ref, v_ref, qm_ref)


def _in_proj_axial_kernel(x_ref, g_ref, w_ref, cr_ref, sr_ref, cc_ref, sc_ref, qkg_ref,
                          q_ref, k_ref, v_ref, qm_ref):
    h = _rms_rows(x_ref[...], g_ref[...]).astype(BF16)
    p = jnp.dot(w_ref[...], h, preferred_element_type=F32)
    cr, sr, cc, sc = cr_ref[...], sr_ref[...], cc_ref[...], sc_ref[...]
    qr = HEAD_DIM // 4

    def rot(head, is_key):
        y = _rms_rows(head, qkg_ref[1] if is_key else qkg_ref[0])
        a1, a2, b1, b2 = y[:qr], y[qr:2 * qr], y[2 * qr:3 * qr], y[3 * qr:]
        return jnp.concatenate(
            [a1 * cr - a2 * sr, a2 * cr + a1 * sr, b1 * cc - b2 * sc, b2 * cc + b1 * sc], axis=0)

    _split_store(p, rot, q_ref, k_ref, v_ref, qm_ref)


def _in_proj(x, g, wt, tables, qk_g, seq_len, entry=False):
    tm = TOKEN_TILE
    t = x.shape[0] if entry else x.shape[1]
    n_seq_tiles = seq_len // tm
    tab_specs = [pl.BlockSpec((tb.shape[0], tm), lambda i: (0, i % n_seq_tiles)) for tb in tables]
    x_spec = pl.BlockSpec((tm, D_MODEL), lambda i: (i, 0)) if entry else pl.BlockSpec((D_MODEL, tm), lambda i: (0, i))
    in_specs = [x_spec, pl.BlockSpec(g.shape, lambda i: (0, 0)), pl.BlockSpec((IN_W, D_MODEL), lambda i: (0, 0))]
    in_specs += tab_specs
    args = [x, g, wt] + list(tables)
    widths, dtypes = (Q_W, KV_W, V_W, QM_W), (BF16,) * 4
    if entry:
        assert qk_g is None
        body = _in_proj_entry_kernel
        widths, dtypes = (D_MODEL,) + widths, (F32,) + dtypes
    elif qk_g is None:
        body = _in_proj_partial_kernel
    else:
        body = _in_proj_axial_kernel
        in_specs.append(pl.BlockSpec((2, HEAD_DIM, 1), lambda i: (0, 0, 0)))
        args.append(qk_g)
    return pl.pallas_call(
        body,
        grid=(t // tm,),
        in_specs=in_specs,
        out_specs=[pl.BlockSpec((w, tm), lambda i: (0, i)) for w in widths],
        out_shape=[jax.ShapeDtypeStruct((w, t), dt) for w, dt in zip(widths, dtypes)],
        compiler_params=_params("parallel"),
        name="in_proj",
    )(*args)


def _sublane_all(x, op):
    for shift in (4, 2, 1):
        x = op(x, pltpu.roll(x, shift, 0))
    return x


def _scores(kc, q, bias=None):
    s = jnp.dot(kc, q, preferred_element_type=F32)
    if bias is not None:
        s = s + bias
    return s.reshape(kc.shape[0] // 8, 8, q.shape[1])


def _softmax_update(s3, vc, m_old, l_old, acc_old):
    n, _, qs = s3.shape
    m_new = jnp.maximum(m_old, _sublane_all(jnp.max(s3, axis=0), jnp.maximum))
    alpha = jnp.exp2(m_old - m_new)
    p = jnp.exp2((s3 - m_new[None]).reshape(n * 8, qs).astype(BF16))
    pv = jnp.dot(vc, p, preferred_element_type=F32)
    l_new = alpha * l_old + pv[HEAD_DIM:HEAD_DIM + 8]
    acc3 = alpha[None] * acc_old.reshape(HEAD_DIM // 8, 8, qs) + pv[:HEAD_DIM].reshape(HEAD_DIM // 8, 8, qs)
    return m_new, l_new, acc3.reshape(HEAD_DIM, qs)


def _fresh_state():
    return (jnp.full((8, Q_SUB), NEG_INIT, F32), jnp.zeros((8, Q_SUB), F32),
            jnp.zeros((HEAD_DIM, Q_SUB), F32))


def _run_units(units, score_fn, update_fn, lookahead, fillers=()):
    pending, fillers = {}, list(fillers)
    for n in range(len(units) + lookahead):
        if n < len(units):
            pending[n] = score_fn(units[n])
        if fillers and n >= lookahead:
            fillers.pop(0)()
        if n >= lookahead:
            update_fn(units[n - lookahead], pending.pop(n - lookahead))
    for f in fillers:
        f()


def _scale_rows(acc, w):
    qs = acc.shape[1]
    return (acc.reshape(HEAD_DIM // 8, 8, qs) * w[None]).reshape(HEAD_DIM, qs)


def _normalise(acc, l):
    return _scale_rows(acc, 1.0 / _sublane_all(l, jnp.add))


def _band_bias(delta0, reach, dil, row_lo=0, row_hi=KEY_CHUNK):
    kr = np.arange(KEY_CHUNK)[:, None]
    delta = delta0 + kr - np.arange(Q_SUB)[None, :]
    ok = (np.abs(delta) <= reach) & (delta % dil == 0) & (kr >= row_lo) & (kr < row_hi)
    return np.where(ok, 0.0, -np.inf).astype(np.float32)


def _window_chunks(reach):
    pad = -(-reach // 128) * 128
    return pad, (Q_SUB + 2 * pad) // KEY_CHUNK


def _window_biases(reach, dil):
    pad, n_chunk = _window_chunks(reach)
    out = [_band_bias(c * KEY_CHUNK - pad, reach, dil) for c in range(n_chunk)]
    if pad % KEY_CHUNK:
        out.append(_band_bias(-pad, reach, dil, row_lo=pad % KEY_CHUNK))
        out.append(_band_bias((n_chunk - 1) * KEY_CHUNK - pad, reach, dil, row_hi=KEY_CHUNK - pad % KEY_CHUNK))
    return out


def _bias_index(start, c, pad, n_chunk, base, masked_index, seq_len):
    if pad % KEY_CHUNK == 0:
        on = (start >= 0) & (start + KEY_CHUNK <= seq_len)
        return jnp.where(on, base + c, masked_index)
    idx = base + c
    if c == 0:
        idx = jnp.where(start < 0, base + n_chunk, idx)
    if c == n_chunk - 1:
        idx = jnp.where(start + KEY_CHUNK > seq_len, base + n_chunk + 1, idx)
    return idx


def _banded_kernel(sink_ref, bias_ref, q_ref, kl_ref, km_ref, kr_ref, vl_ref, vm_ref, vr_ref, o_ref, *,
                   seq_len):
    g = pl.program_id(1)
    q0 = pl.program_id(2) * ATTN_TILE
    pad, n_chunk = _window_chunks(A_RADIUS)
    kwin = jnp.concatenate([kl_ref[...], km_ref[...], kr_ref[...]], axis=1).T
    vwin = jnp.concatenate([vl_ref[...], vm_ref[...], vr_ref[...]], axis=1)
    n_sub = ATTN_TILE // Q_SUB

    first_sublane = lax.broadcasted_iota(jnp.int32, (8, Q_SUB), 0) == 0
    state = {}
    for j in range(GQA_GROUP):
        sink = sink_ref[g * GQA_GROUP + j] * LOG2E
        for u in range(n_sub):
            state[j, u] = (jnp.full((8, Q_SUB), sink, F32), jnp.where(first_sublane, 1.0, 0.0),
                           jnp.zeros((HEAD_DIM, Q_SUB), F32))

    def scores(unit):
        u, c, j = unit
        w0 = u * Q_SUB + c * KEY_CHUNK
        q = q_ref[j * HEAD_DIM:(j + 1) * HEAD_DIM, u * Q_SUB:(u + 1) * Q_SUB]
        idx = _bias_index(q0 + w0 - pad, c, pad, n_chunk, 0, None, seq_len)
        return _scores(kwin[w0:w0 + KEY_CHUNK], q, bias_ref[idx])

    def update(unit, s3):
        u, c, j = unit
        w0 = u * Q_SUB + c * KEY_CHUNK
        state[j, u] = _softmax_update(s3, vwin[:, w0:w0 + KEY_CHUNK], *state[j, u])

    units = [(u, c, j) for u in range(n_sub) for c in range(n_chunk) for j in range(GQA_GROUP)]
    _run_units(units, scores, update, UNIT_LOOKAHEAD)

    for (j, u), (_, l, acc) in state.items():
        o_ref[j * HEAD_DIM:(j + 1) * HEAD_DIM, u * Q_SUB:(u + 1) * Q_SUB] = _normalise(acc, l).astype(BF16)


def _banded_attention(sink, qt, kt, vt, batch, seq_len):
    tq = ATTN_TILE
    nq = seq_len // tq
    pad, _ = _window_chunks(A_RADIUS)
    halo_per_tile = tq // pad
    n_halo = seq_len // pad
    bias = jnp.asarray(np.stack(_window_biases(A_RADIUS, 1)))

    def left(b, g, i):
        return (g, b * n_halo + jnp.maximum(i * halo_per_tile - 1, 0))

    def right(b, g, i):
        return (g, b * n_halo + jnp.minimum((i + 1) * halo_per_tile, n_halo - 1))

    def main(b, g, i):
        return (g, b * nq + i)

    def kv_specs(rows):
        return [pl.BlockSpec((rows, pad), left), pl.BlockSpec((rows, tq), main), pl.BlockSpec((rows, pad), right)]

    return pl.pallas_call(
        functools.partial(_banded_kernel, seq_len=seq_len),
        grid=(batch, N_KV_HEADS, nq),
        in_specs=[
            pl.BlockSpec(memory_space=pltpu.SMEM),
            pl.BlockSpec(bias.shape, lambda b, g, i: (0, 0, 0)),
            pl.BlockSpec((GQA_GROUP * HEAD_DIM, tq), main),
        ] + kv_specs(HEAD_DIM) + kv_specs(V_ROWS),
        out_specs=pl.BlockSpec((GQA_GROUP * HEAD_DIM, tq), main),
        out_shape=jax.ShapeDtypeStruct(qt.shape, BF16),
        compiler_params=_params("parallel", "parallel", "parallel"),
        name="banded_attention",
    )(sink, bias, qt, kt, kt, kt, vt, vt, vt)


def _flash_kernel(q_ref, k_ref, v_ref, o_ref, k_sc, m_sc, l_sc, acc_sc, *, seq_len):
    tq = q_ref.shape[1]

    @pl.when(pl.program_id(2) == 0)
    def _():
        def fill(c, carry):
            off = pl.multiple_of(c * ATTN_TILE, ATTN_TILE)
            k_sc[pl.ds(off, ATTN_TILE), :] = k_ref[:, pl.ds(off, ATTN_TILE)].T
            return carry
        lax.fori_loop(0, seq_len // ATTN_TILE, fill, 0)

    m_sc[...] = jnp.full(m_sc.shape, NEG_INIT, F32)
    l_sc[...] = jnp.zeros(l_sc.shape, F32)
    acc_sc[...] = jnp.zeros(acc_sc.shape, F32)

    subs = [(j, u) for j in range(GQA_GROUP) for u in range(tq // Q_SUB)]
    units = [(c, j, u) for c in range(FLASH_CHUNKS) for (j, u) in subs]

    def body(it, carry):
        kcs, vcs = [], []
        for c in range(FLASH_CHUNKS):
            off = pl.multiple_of((it * FLASH_CHUNKS + c) * KEY_CHUNK, KEY_CHUNK)
            kcs.append(k_sc[pl.ds(off, KEY_CHUNK), :])
            vcs.append(v_ref[:, pl.ds(off, KEY_CHUNK)])
        state = {}
        for j, u in subs:
            cols = slice(u * Q_SUB, (u + 1) * Q_SUB)
            state[j, u] = (m_sc[j, :, cols], l_sc[j, :, cols], acc_sc[j, :, cols])

        def scores(unit):
            c, j, u = unit
            return _scores(kcs[c], q_ref[j * HEAD_DIM:(j + 1) * HEAD_DIM, u * Q_SUB:(u + 1) * Q_SUB])

        def update(unit, s3):
            c, j, u = unit
            state[j, u] = _softmax_update(s3, vcs[c], *state[j, u])

        _run_units(units, scores, update, UNIT_LOOKAHEAD)
        for j, u in subs:
            cols = slice(u * Q_SUB, (u + 1) * Q_SUB)
            m_sc[j, :, cols], l_sc[j, :, cols], acc_sc[j, :, cols] = state[j, u]
        return carry

    lax.fori_loop(0, seq_len // (KEY_CHUNK * FLASH_CHUNKS), body, 0)

    for j in range(GQA_GROUP):
        o_ref[j * HEAD_DIM:(j + 1) * HEAD_DIM, :] = _normalise(acc_sc[j], l_sc[j]).astype(BF16)


def _full_attention(qt, kt, vt, batch, seq_len):
    tq = ATTN_TILE
    nq = seq_len // tq
    return pl.pallas_call(
        functools.partial(_flash_kernel, seq_len=seq_len),
        grid=(batch, N_KV_HEADS, nq),
        in_specs=[
            pl.BlockSpec((GQA_GROUP * HEAD_DIM, tq), lambda b, g, i: (g, b * nq + i)),
            pl.BlockSpec((HEAD_DIM, seq_len), lambda b, g, i: (g, b)),
            pl.BlockSpec((V_ROWS, seq_len), lambda b, g, i: (g, b)),
        ],
        out_specs=pl.BlockSpec((GQA_GROUP * HEAD_DIM, tq), lambda b, g, i: (g, b * nq + i)),
        out_shape=jax.ShapeDtypeStruct(qt.shape, BF16),
        scratch_shapes=[
            pltpu.VMEM((seq_len, HEAD_DIM), BF16),
            pltpu.VMEM((GQA_GROUP, 8, tq), F32),
            pltpu.VMEM((GQA_GROUP, 8, tq), F32),
            pltpu.VMEM((GQA_GROUP, HEAD_DIM, tq), F32),
        ],
        compiler_params=_params("parallel", "parallel", "arbitrary"),
        name="full_attention",
    )(qt, kt, vt)


C_BLOCKS = 5


def _dilated_bias_table():
    blocks, bases = [], []
    for _, dil in C_GROUPS:
        bases.append(len(blocks))
        blocks += _window_biases(C_RADIUS * dil, dil)
    blocks.append(np.full((KEY_CHUNK, Q_SUB), -np.inf, np.float32))
    return np.stack(blocks), bases


def _dilated_kernel(bias_ref, q_ref, *refs, seq_len, bases):
    k_refs, v_refs, o_ref = refs[:C_BLOCKS], refs[C_BLOCKS:2 * C_BLOCKS], refs[2 * C_BLOCKS]
    q0 = pl.program_id(1) * ATTN_TILE
    centre = (C_BLOCKS // 2) * ATTN_TILE
    kwin = jnp.concatenate([r[...] for r in k_refs], axis=1)
    vwin = jnp.concatenate([r[...] for r in v_refs], axis=1)
    masked_index = bias_ref.shape[0] - 1
    n_sub = ATTN_TILE // Q_SUB

    geom, keys, units, state = {}, {}, [], {}
    for g, (_, dil) in enumerate(C_GROUPS):
        pad, n_chunk = _window_chunks(C_RADIUS * dil)
        geom[g] = (pad, n_chunk)
        assert centre - pad >= 0 and centre + ATTN_TILE + pad <= C_BLOCKS * ATTN_TILE
        keys[g] = kwin[g * HEAD_DIM:(g + 1) * HEAD_DIM, centre - pad:centre + ATTN_TILE + pad].T
        units += [(g, u, c, j) for u in range(n_sub) for c in range(n_chunk) for j in range(GQA_GROUP)]
        for j in range(GQA_GROUP):
            for u in range(n_sub):
                state[g, j, u] = _fresh_state()

    def scores(unit):
        g, u, c, j = unit
        pad, n_chunk = geom[g]
        w0 = u * Q_SUB + c * KEY_CHUNK
        head = g * GQA_GROUP + j
        q = q_ref[head * HEAD_DIM:(head + 1) * HEAD_DIM, u * Q_SUB:(u + 1) * Q_SUB]
        idx = _bias_index(q0 + w0 - pad, c, pad, n_chunk, bases[g], masked_index, seq_len)
        return _scores(keys[g][w0:w0 + KEY_CHUNK], q, bias_ref[idx])

    def update(unit, s3):
        g, u, c, j = unit
        col = centre - geom[g][0] + u * Q_SUB + c * KEY_CHUNK
        vc = vwin[g * V_ROWS:(g + 1) * V_ROWS, col:col + KEY_CHUNK]
        state[g, j, u] = _softmax_update(s3, vc, *state[g, j, u])

    _run_units(units, scores, update, UNIT_LOOKAHEAD)

    groups = range(len(C_GROUPS))
    for j in range(GQA_GROUP):
        for u in range(n_sub):
            mx = functools.reduce(jnp.maximum, [state[g, j, u][0] for g in groups])
            es = [jnp.exp2(state[g, j, u][0] - mx) for g in groups]
            tot = functools.reduce(
                lambda a, b: a + b, [_sublane_all(state[g, j, u][1], jnp.add) * es[g] for g in groups])
            for g in groups:
                r = (g * GQA_GROUP + j) * HEAD_DIM
                o_ref[r:r + HEAD_DIM, u * Q_SUB:(u + 1) * Q_SUB] = _scale_rows(
                    state[g, j, u][2], es[g] / tot).astype(BF16)


def _dilated_attention(qt, kt, vt, batch, seq_len):
    tq = ATTN_TILE
    nq = seq_len // tq
    table, bases = _dilated_bias_table()
    bias = jnp.asarray(table)

    def block(rows, off):
        return pl.BlockSpec((rows, tq), lambda b, i: (0, b * nq + jnp.clip(i + off, 0, nq - 1)))

    offs = range(-(C_BLOCKS // 2), C_BLOCKS // 2 + 1)
    return pl.pallas_call(
        functools.partial(_dilated_kernel, seq_len=seq_len, bases=tuple(bases)),
        grid=(batch, nq),
        in_specs=[pl.BlockSpec(bias.shape, lambda b, i: (0, 0, 0)),
                  pl.BlockSpec((Q_W, tq), lambda b, i: (0, b * nq + i))]
        + [block(KV_W, o) for o in offs] + [block(V_W, o) for o in offs],
        out_specs=pl.BlockSpec((Q_W, tq), lambda b, i: (0, b * nq + i)),
        out_shape=jax.ShapeDtypeStruct(qt.shape, BF16),
        compiler_params=_params("parallel", "parallel"),
        name="dilated_attention",
    )(bias, qt, *([kt] * C_BLOCKS), *([vt] * C_BLOCKS))


def _out_proj_kernel(x_ref, tok_ref, qm_ref, km_ref, vmt_ref, w_ref, g_ref, o_ref):
    tm = x_ref.shape[1]
    mem_out = {}

    def scores(unit):
        h, u = unit
        return _scores(km_ref[h], qm_ref[h * HEAD_DIM:(h + 1) * HEAD_DIM, u * Q_SUB:(u + 1) * Q_SUB])

    def update(unit, s3):
        _, l, acc = _softmax_update(s3, vmt_ref[unit[0]], *_fresh_state())
        mem_out[unit] = _normalise(acc, l).astype(BF16)

    tok_out = {}

    def tok_piece(r, u):
        def piece():
            tok_out[r, u] = jnp.dot(w_ref[r:r + OUT_ROWS, :Q_W], tok_ref[:, u * Q_SUB:(u + 1) * Q_SUB],
                                    preferred_element_type=F32)
        return piece

    n_sub = tm // Q_SUB
    row_blocks = range(0, D_MODEL, OUT_ROWS)
    units = [(h, u) for h in range(N_MEM_HEADS) for u in range(n_sub)]
    _run_units(units, scores, update, UNIT_LOOKAHEAD,
               fillers=[tok_piece(r, u) for u in range(n_sub) for r in row_blocks])
    mo = jnp.concatenate(
        [jnp.concatenate([mem_out[h, u] for u in range(n_sub)], axis=1) for h in range(N_MEM_HEADS)],
        axis=0)
    y_tok = jnp.concatenate(
        [jnp.concatenate([tok_out[r, u] for r in row_blocks], axis=0) for u in range(n_sub)], axis=1)
    y = y_tok + jnp.dot(w_ref[:, Q_W:], mo, preferred_element_type=F32)
    o_ref[...] = x_ref[...] + _rms_rows(y, g_ref[...])


def _out_proj(xt, tok_t, qm_t, km, vmt, wt, g_col, seq_len):
    t = xt.shape[1]
    tm = OUT_TILE
    per_seq = seq_len // tm
    return pl.pallas_call(
        _out_proj_kernel,
        grid=(t // tm,),
        in_specs=[
            pl.BlockSpec((D_MODEL, tm), lambda i: (0, i)),
            pl.BlockSpec((Q_W, tm), lambda i: (0, i)),
            pl.BlockSpec((QM_W, tm), lambda i: (0, i)),
            pl.BlockSpec((None, N_MEM_HEADS, N_MEM, HEAD_DIM), lambda i: (i // per_seq, 0, 0, 0)),
            pl.BlockSpec((None, N_MEM_HEADS, V_ROWS, N_MEM), lambda i: (i // per_seq, 0, 0, 0)),
            pl.BlockSpec((D_MODEL, MIX_WIDTH), lambda i: (0, 0)),
            pl.BlockSpec((D_MODEL, 1), lambda i: (0, 0)),
        ],
        out_specs=pl.BlockSpec((D_MODEL, tm), lambda i: (0, i)),
        out_shape=jax.ShapeDtypeStruct(xt.shape, F32),
        compiler_params=_params("parallel"),
        name="out_proj",
    )(xt, tok_t, qm_t, km, vmt, wt, g_col)


def _ffn_kernel(x_ref, gpre_ref, wgu_ref, wd_ref, gpost_ref, o_ref, *, token_major_out):
    h = _rms_rows(x_ref[...], gpre_ref[...]).astype(BF16)
    gu = jnp.dot(wgu_ref[...], h, preferred_element_type=F32)
    gate, up = gu[:D_FF], gu[D_FF:]
    act = (gate * jax.nn.sigmoid(gate) * up).astype(BF16)
    y = jnp.dot(wd_ref[...], act, preferred_element_type=F32)
    out = x_ref[...] + _rms_rows(y, gpost_ref[...])
    o_ref[...] = out.T if token_major_out else out


def _ffn(xt, gpre, wgu_t, wd_t, gpost, token_major_out=False):
    d, t = xt.shape
    tm = TOKEN_TILE
    resident = pl.Buffered(1)
    if token_major_out:
        out_spec, out_shape = pl.BlockSpec((tm, d), lambda i: (i, 0)), (t, d)
    else:
        out_spec, out_shape = pl.BlockSpec((d, tm), lambda i: (0, i)), (d, t)
    return pl.pallas_call(
        functools.partial(_ffn_kernel, token_major_out=token_major_out),
        grid=(t // tm,),
        in_specs=[
            pl.BlockSpec((d, tm), lambda i: (0, i)),
            pl.BlockSpec((d, 1), lambda i: (0, 0)),
            pl.BlockSpec((2 * D_FF, d), lambda i: (0, 0), pipeline_mode=resident),
            pl.BlockSpec((d, D_FF), lambda i: (0, 0), pipeline_mode=resident),
            pl.BlockSpec((d, 1), lambda i: (0, 0)),
        ],
        out_specs=out_spec,
        out_shape=jax.ShapeDtypeStruct(out_shape, F32),
        compiler_params=_params("parallel"),
        name="ffn",
    )(xt, gpre, wgu_t, wd_t, gpost)


def _rope_table_t(pos, n_dims, theta):
    inv = theta ** (-(jnp.arange(0, n_dims, 2, dtype=F32) / n_dims))
    ang = inv[:, None] * pos.astype(F32)[None, :]
    return jnp.cos(ang), jnp.sin(ang)


def kernel(x, mem, mem_norm_g, w_in, w_mem_kv, w_o, g_mix_pre, g_mix_post, attn_sink, qk_norm_g,
           w_gate_up, w_down, g_ffn_pre, g_ffn_post):
    batch, seq_len, _ = x.shape
    assert seq_len % (KEY_CHUNK * FLASH_CHUNKS) == 0 and seq_len % OUT_TILE == 0
    pos = jnp.arange(seq_len, dtype=jnp.int32)
    partial_tables = _rope_table_t(pos, ROPE_DIMS, ROPE_THETA)
    axial_tables = (_rope_table_t(pos // GRID_W, HEAD_DIM // 2, AXIAL_THETA)
                    + _rope_table_t(pos % GRID_W, HEAD_DIM // 2, AXIAL_THETA))

    w_in_t = jnp.swapaxes(w_in, 1, 2).astype(BF16)
    w_o_t = jnp.swapaxes(w_o, 1, 2).astype(BF16)
    w_gu_t = jnp.swapaxes(w_gate_up, 1, 2).astype(BF16)
    w_d_t = jnp.swapaxes(w_down, 1, 2).astype(BF16)
    w_mem = w_mem_kv.astype(BF16).reshape(DEPTH, D_MODEL, 2, N_MEM_HEADS, HEAD_DIM)
    w_mk = jnp.transpose(w_mem[:, :, 0], (0, 2, 1, 3))
    w_mv_t = jnp.transpose(w_mem[:, :, 1], (0, 2, 3, 1))

    km, vmt = _mem_kv(mem, mem_norm_g.reshape(1, D_MODEL), w_mk, w_mv_t)

    xt = None
    for i in range(DEPTH):
        kind = i % N_MIXERS
        g_pre = g_mix_pre[i].reshape(D_MODEL, 1)
        if kind == 1:
            qk_g = qk_norm_g[i // N_MIXERS].reshape(2, HEAD_DIM, 1)
            qt, kt, vt, qmt = _in_proj(xt, g_pre, w_in_t[i], axial_tables, qk_g, seq_len)
            tok_t = _full_attention(qt, kt, vt, batch, seq_len)
        else:
            if i == 0:
                xt, qt, kt, vt, qmt = _in_proj(x.reshape(batch * seq_len, D_MODEL), g_pre.reshape(1, D_MODEL),
                                               w_in_t[i], partial_tables, None, seq_len, entry=True)
            else:
                qt, kt, vt, qmt = _in_proj(xt, g_pre, w_in_t[i], partial_tables, None, seq_len)
            if kind == 0:
                tok_t = _banded_attention(attn_sink[i // N_MIXERS], qt, kt, vt, batch, seq_len)
            else:
                tok_t = _dilated_attention(qt, kt, vt, batch, seq_len)
        xt = _out_proj(xt, tok_t, qmt, km[i], vmt[i], w_o_t[i], g_mix_post[i].reshape(D_MODEL, 1), seq_len)
        xt = _ffn(xt, g_ffn_pre[i].reshape(D_MODEL, 1), w_gu_t[i], w_d_t[i], g_ffn_post[i].reshape(D_MODEL, 1),
                  token_major_out=(i == DEPTH - 1))
    return xt.reshape(batch, seq_len, D_MODEL)
```

```python
import functools

import numpy as np
import jax
import jax.numpy as jnp
from jax import lax
from jax.experimental import pallas as pl
from jax.experimental.pallas import tpu as pltpu

D_MODEL = 1024
DEPTH = 4
HEAD_DIM = 64
N_TOK_HEADS = 12
N_KV_HEADS = 3
GQA_GROUP = N_TOK_HEADS // N_KV_HEADS
N_MEM_HEADS = 4
N_MEM = 256
Q_W = N_TOK_HEADS * HEAD_DIM
KV_W = N_KV_HEADS * HEAD_DIM
QM_W = N_MEM_HEADS * HEAD_DIM
IN_W = Q_W + 2 * KV_W + QM_W
MIX_WIDTH = Q_W + QM_W
D_FF = -(-8 * D_MODEL // (3 * 256)) * 256
N_MIXERS = 3
A_RADIUS = 128
C_GROUPS = ((128, 1), (512, 4), (2048, 16))
C_RADIUS = 64
ROPE_THETA = 500000.0
ROPE_DIMS = HEAD_DIM // 4
AXIAL_THETA = 10000.0
GRID_W = 64
EPS = 1e-6
ATTN_SCALE = HEAD_DIM ** -0.5
LOG2E = 1.4426950408889634
Q_SCALE = ATTN_SCALE * LOG2E
NEG_INIT = -1e30

F32 = jnp.float32
BF16 = jnp.bfloat16
BF16_ROWS = 16

V_ROWS = HEAD_DIM + BF16_ROWS
V_W = N_KV_HEADS * V_ROWS

TOKEN_TILE = 512
OUT_TILE = 1024
OUT_ROWS = 256
FF_BLOCK = 256
ATTN_TILE = 512
Q_SUB = 256
KEY_CHUNK = 256
FLASH_CHUNKS = 16
UNIT_LOOKAHEAD = 5
VMEM_LIMIT = 56 * 1024 * 1024

_NT = (((1,), (1,)), ((), ()))


def _params(*semantics):
    return pltpu.CompilerParams(dimension_semantics=semantics, vmem_limit_bytes=VMEM_LIMIT)


def _rms_rows(x, g_col):
    ms = jnp.mean(x * x, axis=0, keepdims=True)
    return x * lax.rsqrt(ms + EPS) * g_col


def _ones_tile(n):
    return jnp.where(lax.broadcasted_iota(jnp.int32, (BF16_ROWS, n), 0) == 0, 1.0, 0.0).astype(BF16)


def _mem_kv_kernel(mem_ref, g_ref, wk_ref, wvt_ref, km_ref, vmt_ref):
    m = mem_ref[...]
    ms = jnp.mean(m * m, axis=-1, keepdims=True)
    mn = (m * lax.rsqrt(ms + EPS) * g_ref[...]).astype(BF16)
    for h in range(N_MEM_HEADS):
        km_ref[h] = jnp.dot(mn, wk_ref[h], preferred_element_type=F32).astype(BF16)
        vmt_ref[h, :HEAD_DIM] = lax.dot_general(wvt_ref[h], mn, _NT, preferred_element_type=F32).astype(BF16)
        vmt_ref[h, HEAD_DIM:] = _ones_tile(N_MEM)


def _mem_kv(mem, g_row, wk, wvt):
    b = mem.shape[0]
    return pl.pallas_call(
        _mem_kv_kernel,
        grid=(DEPTH, b),
        in_specs=[
            pl.BlockSpec((None, N_MEM, D_MODEL), lambda l, i: (i, 0, 0)),
            pl.BlockSpec((1, D_MODEL), lambda l, i: (0, 0)),
            pl.BlockSpec((None, N_MEM_HEADS, D_MODEL, HEAD_DIM), lambda l, i: (l, 0, 0, 0)),
            pl.BlockSpec((None, N_MEM_HEADS, HEAD_DIM, D_MODEL), lambda l, i: (l, 0, 0, 0)),
        ],
        out_specs=[
            pl.BlockSpec((None, None, N_MEM_HEADS, N_MEM, HEAD_DIM), lambda l, i: (l, i, 0, 0, 0)),
            pl.BlockSpec((None, None, N_MEM_HEADS, V_ROWS, N_MEM), lambda l, i: (l, i, 0, 0, 0)),
        ],
        out_shape=[
            jax.ShapeDtypeStruct((DEPTH, b, N_MEM_HEADS, N_MEM, HEAD_DIM), BF16),
            jax.ShapeDtypeStruct((DEPTH, b, N_MEM_HEADS, V_ROWS, N_MEM), BF16),
        ],
        compiler_params=_params("parallel", "parallel"),
        name="mem_kv",
    )(mem, g_row, wk, wvt)


def _split_store(p, rot, q_ref, k_ref, v_ref, qm_ref):
    for hd in range(N_TOK_HEADS):
        r = hd * HEAD_DIM
        q_ref[r:r + HEAD_DIM, :] = (rot(p[r:r + HEAD_DIM], False) * Q_SCALE).astype(BF16)
    ones = _ones_tile(p.shape[1])
    for hd in range(N_KV_HEADS):
        r = hd * HEAD_DIM
        k_ref[r:r + HEAD_DIM, :] = rot(p[Q_W + r:Q_W + r + HEAD_DIM], True).astype(BF16)
        v_ref[hd * V_ROWS:hd * V_ROWS + HEAD_DIM, :] = p[Q_W + KV_W + r:Q_W + KV_W + r + HEAD_DIM].astype(BF16)
        v_ref[hd * V_ROWS + HEAD_DIM:(hd + 1) * V_ROWS, :] = ones
    qm_ref[...] = (p[Q_W + 2 * KV_W:] * Q_SCALE).astype(BF16)


def _partial_rope_store(p, cos_ref, sin_ref, q_ref, k_ref, v_ref, qm_ref):
    c, s = cos_ref[...], sin_ref[...]
    half = ROPE_DIMS // 2

    def rot(head, is_key):
        x1, x2 = head[:half], head[half:ROPE_DIMS]
        return jnp.concatenate([x1 * c - x2 * s, x2 * c + x1 * s, head[ROPE_DIMS:]], axis=0)

    _split_store(p, rot, q_ref, k_ref, v_ref, qm_ref)


def _in_proj_partial_kernel(x_ref, g_ref, w_ref, cos_ref, sin_ref, q_ref, k_ref, v_ref, qm_ref):
    h = _rms_rows(x_ref[...], g_ref[...]).astype(BF16)
    p = jnp.dot(w_ref[...], h, preferred_element_type=F32)
    _partial_rope_store(p, cos_ref, sin_ref, q_ref, k_ref, v_ref, qm_ref)


def _in_proj_entry_kernel(x_ref, g_ref, w_ref, cos_ref, sin_ref, xt_ref, q_ref, k_ref, v_ref, qm_ref):
    x = x_ref[...]
    ms = jnp.mean(x * x, axis=-1, keepdims=True)
    h = (x * lax.rsqrt(ms + EPS) * g_ref[...]).astype(BF16)
    p = lax.dot_general(w_ref[...], h, _NT, preferred_element_type=F32)
    xt_ref[...] = x.T
    _partial_rope_store(p, cos_ref, sin_ref, q_ref, k_ref, v_ref, qm_ref)


def _in_proj_axial_kernel(x_ref, g_ref, w_ref, cr_ref, sr_ref, cc_ref, sc_ref, qkg_ref,
                          q_ref, k_ref, v_ref, qm_ref):
    h = _rms_rows(x_ref[...], g_ref[...]).astype(BF16)
    p = jnp.dot(w_ref[...], h, preferred_element_type=F32)
    cr, sr, cc, sc = cr_ref[...], sr_ref[...], cc_ref[...], sc_ref[...]
    qr = HEAD_DIM // 4

    def rot(head, is_key):
        y = _rms_rows(head, qkg_ref[1] if is_key else qkg_ref[0])
        a1, a2, b1, b2 = y[:qr], y[qr:2 * qr], y[2 * qr:3 * qr], y[3 * qr:]
        return jnp.concatenate(
            [a1 * cr - a2 * sr, a2 * cr + a1 * sr, b1 * cc - b2 * sc, b2 * cc + b1 * sc], axis=0)

    _split_store(p, rot, q_ref, k_ref, v_ref, qm_ref)


def _in_proj(x, g, wt, tables, qk_g, seq_len, entry=False):
    tm = TOKEN_TILE
    t = x.shape[0] if entry else x.shape[1]
    n_seq_tiles = seq_len // tm
    tab_specs = [pl.BlockSpec((tb.shape[0], tm), lambda i: (0, i % n_seq_tiles)) for tb in tables]
    x_spec = pl.BlockSpec((tm, D_MODEL), lambda i: (i, 0)) if entry else pl.BlockSpec((D_MODEL, tm), lambda i: (0, i))
    in_specs = [x_spec, pl.BlockSpec(g.shape, lambda i: (0, 0)), pl.BlockSpec((IN_W, D_MODEL), lambda i: (0, 0))]
    in_specs += tab_specs
    args = [x, g, wt] + list(tables)
    widths, dtypes = (Q_W, KV_W, V_W, QM_W), (BF16,) * 4
    if entry:
        assert qk_g is None
        body = _in_proj_entry_kernel
        widths, dtypes = (D_MODEL,) + widths, (F32,) + dtypes
    elif qk_g is None:
        body = _in_proj_partial_kernel
    else:
        body = _in_proj_axial_kernel
        in_specs.append(pl.BlockSpec((2, HEAD_DIM, 1), lambda i: (0, 0, 0)))
        args.append(qk_g)
    return pl.pallas_call(
        body,
        grid=(t // tm,),
        in_specs=in_specs,
        out_specs=[pl.BlockSpec((w, tm), lambda i: (0, i)) for w in widths],
        out_shape=[jax.ShapeDtypeStruct((w, t), dt) for w, dt in zip(widths, dtypes)],
        compiler_params=_params("parallel"),
        name="in_proj",
    )(*args)


def _sublane_all(x, op):
    for shift in (4, 2, 1):
        x = op(x, pltpu.roll(x, shift, 0))
    return x


def _scores(kc, q, bias=None):
    s = jnp.dot(kc, q, preferred_element_type=F32)
    if bias is not None:
        s = s + bias
    return s.reshape(kc.shape[0] // 8, 8, q.shape[1])


def _softmax_update(s3, vc, m_old, l_old, acc_old):
    n, _, qs = s3.shape
    m_new = jnp.maximum(m_old, _sublane_all(jnp.max(s3, axis=0), jnp.maximum))
    alpha = jnp.exp2(m_old - m_new)
    p = jnp.exp2((s3 - m_new[None]).reshape(n * 8, qs).astype(BF16))
    pv = jnp.dot(vc, p, preferred_element_type=F32)
    l_new = alpha * l_old + pv[HEAD_DIM:HEAD_DIM + 8]
    acc3 = alpha[None] * acc_old.reshape(HEAD_DIM // 8, 8, qs) + pv[:HEAD_DIM].reshape(HEAD_DIM // 8, 8, qs)
    return m_new, l_new, acc3.reshape(HEAD_DIM, qs)


def _fresh_state():
    return (jnp.full((8, Q_SUB), NEG_INIT, F32), jnp.zeros((8, Q_SUB), F32),
            jnp.zeros((HEAD_DIM, Q_SUB), F32))


def _run_units(units, score_fn, update_fn, lookahead, fillers=()):
    pending, fillers = {}, list(fillers)
    for n in range(len(units) + lookahead):
        if n < len(units):
            pending[n] = score_fn(units[n])
        if fillers and n >= lookahead:
            fillers.pop(0)()
        if n >= lookahead:
            update_fn(units[n - lookahead], pending.pop(n - lookahead))
    for f in fillers:
        f()


def _scale_rows(acc, w):
    qs = acc.shape[1]
    return (acc.reshape(HEAD_DIM // 8, 8, qs) * w[None]).reshape(HEAD_DIM, qs)


def _normalise(acc, l):
    return _scale_rows(acc, 1.0 / _sublane_all(l, jnp.add))


def _band_bias(delta0, reach, dil, row_lo=0, row_hi=KEY_CHUNK):
    kr = np.arange(KEY_CHUNK)[:, None]
    delta = delta0 + kr - np.arange(Q_SUB)[None, :]
    ok = (np.abs(delta) <= reach) & (delta % dil == 0) & (kr >= row_lo) & (kr < row_hi)
    return np.where(ok, 0.0, -np.inf).astype(np.float32)


def _window_chunks(reach):
    pad = -(-reach // 128) * 128
    return pad, (Q_SUB + 2 * pad) // KEY_CHUNK


def _window_biases(reach, dil):
    pad, n_chunk = _window_chunks(reach)
    out = [_band_bias(c * KEY_CHUNK - pad, reach, dil) for c in range(n_chunk)]
    if pad % KEY_CHUNK:
        out.append(_band_bias(-pad, reach, dil, row_lo=pad % KEY_CHUNK))
        out.append(_band_bias((n_chunk - 1) * KEY_CHUNK - pad, reach, dil, row_hi=KEY_CHUNK - pad % KEY_CHUNK))
    return out


def _bias_index(start, c, pad, n_chunk, base, masked_index, seq_len):
    if pad % KEY_CHUNK == 0:
        on = (start >= 0) & (start + KEY_CHUNK <= seq_len)
        return jnp.where(on, base + c, masked_index)
    idx = base + c
    if c == 0:
        idx = jnp.where(start < 0, base + n_chunk, idx)
    if c == n_chunk - 1:
        idx = jnp.where(start + KEY_CHUNK > seq_len, base + n_chunk + 1, idx)
    return idx


def _banded_kernel(sink_ref, bias_ref, q_ref, kl_ref, km_ref, kr_ref, vl_ref, vm_ref, vr_ref, o_ref, *,
                   seq_len):
    g = pl.program_id(1)
    q0 = pl.program_id(2) * ATTN_TILE
    pad, n_chunk = _window_chunks(A_RADIUS)
    kwin = jnp.concatenate([kl_ref[...], km_ref[...], kr_ref[...]], axis=1).T
    vwin = jnp.concatenate([vl_ref[...], vm_ref[...], vr_ref[...]], axis=1)
    n_sub = ATTN_TILE // Q_SUB

    first_sublane = lax.broadcasted_iota(jnp.int32, (8, Q_SUB), 0) == 0
    state = {}
    for j in range(GQA_GROUP):
        sink = sink_ref[g * GQA_GROUP + j] * LOG2E
        for u in range(n_sub):
            state[j, u] = (jnp.full((8, Q_SUB), sink, F32), jnp.where(first_sublane, 1.0, 0.0),
                           jnp.zeros((HEAD_DIM, Q_SUB), F32))

    def scores(unit):
        u, c, j = unit
        w0 = u * Q_SUB + c * KEY_CHUNK
        q = q_ref[j * HEAD_DIM:(j + 1) * HEAD_DIM, u * Q_SUB:(u + 1) * Q_SUB]
        idx = _bias_index(q0 + w0 - pad, c, pad, n_chunk, 0, None, seq_len)
        return _scores(kwin[w0:w0 + KEY_CHUNK], q, bias_ref[idx])

    def update(unit, s3):
        u, c, j = unit
        w0 = u * Q_SUB + c * KEY_CHUNK
        state[j, u] = _softmax_update(s3, vwin[:, w0:w0 + KEY_CHUNK], *state[j, u])

    units = [(u, c, j) for u in range(n_sub) for c in range(n_chunk) for j in range(GQA_GROUP)]
    _run_units(units, scores, update, UNIT_LOOKAHEAD)

    for (j, u), (_, l, acc) in state.items():
        o_ref[j * HEAD_DIM:(j + 1) * HEAD_DIM, u * Q_SUB:(u + 1) * Q_SUB] = _normalise(acc, l).astype(BF16)


def _banded_attention(sink, qt, kt, vt, batch, seq_len):
    tq = ATTN_TILE
    nq = seq_len // tq
    pad, _ = _window_chunks(A_RADIUS)
    halo_per_tile = tq // pad
    n_halo = seq_len // pad
    bias = jnp.asarray(np.stack(_window_biases(A_RADIUS, 1)))

    def left(b, g, i):
        return (g, b * n_halo + jnp.maximum(i * halo_per_tile - 1, 0))

    def right(b, g, i):
        return (g, b * n_halo + jnp.minimum((i + 1) * halo_per_tile, n_halo - 1))

    def main(b, g, i):
        return (g, b * nq + i)

    def kv_specs(rows):
        return [pl.BlockSpec((rows, pad), left), pl.BlockSpec((rows, tq), main), pl.BlockSpec((rows, pad), right)]

    return pl.pallas_call(
        functools.partial(_banded_kernel, seq_len=seq_len),
        grid=(batch, N_KV_HEADS, nq),
        in_specs=[
            pl.BlockSpec(memory_space=pltpu.SMEM),
            pl.BlockSpec(bias.shape, lambda b, g, i: (0, 0, 0)),
            pl.BlockSpec((GQA_GROUP * HEAD_DIM, tq), main),
        ] + kv_specs(HEAD_DIM) + kv_specs(V_ROWS),
        out_specs=pl.BlockSpec((GQA_GROUP * HEAD_DIM, tq), main),
        out_shape=jax.ShapeDtypeStruct(qt.shape, BF16),
        compiler_params=_params("parallel", "parallel", "parallel"),
        name="banded_attention",
    )(sink, bias, qt, kt, kt, kt, vt, vt, vt)


def _flash_kernel(q_ref, k_ref, v_ref, o_ref, k_sc, m_sc, l_sc, acc_sc, *, seq_len):
    tq = q_ref.shape[1]

    @pl.when(pl.program_id(2) == 0)
    def _():
        def fill(c, carry):
            off = pl.multiple_of(c * ATTN_TILE, ATTN_TILE)
            k_sc[pl.ds(off, ATTN_TILE), :] = k_ref[:, pl.ds(off, ATTN_TILE)].T
            return carry
        lax.fori_loop(0, seq_len // ATTN_TILE, fill, 0)

    m_sc[...] = jnp.full(m_sc.shape, NEG_INIT, F32)
    l_sc[...] = jnp.zeros(l_sc.shape, F32)
    acc_sc[...] = jnp.zeros(acc_sc.shape, F32)

    subs = [(j, u) for j in range(GQA_GROUP) for u in range(tq // Q_SUB)]
    units = [(c, j, u) for c in range(FLASH_CHUNKS) for (j, u) in subs]

    def body(it, carry):
        kcs, vcs = [], []
        for c in range(FLASH_CHUNKS):
            off = pl.multiple_of((it * FLASH_CHUNKS + c) * KEY_CHUNK, KEY_CHUNK)
            kcs.append(k_sc[pl.ds(off, KEY_CHUNK), :])
            vcs.append(v_ref[:, pl.ds(off, KEY_CHUNK)])
        state = {}
        for j, u in subs:
            cols = slice(u * Q_SUB, (u + 1) * Q_SUB)
            state[j, u] = (m_sc[j, :, cols], l_sc[j, :, cols], acc_sc[j, :, cols])

        def scores(unit):
            c, j, u = unit
            return _scores(kcs[c], q_ref[j * HEAD_DIM:(j + 1) * HEAD_DIM, u * Q_SUB:(u + 1) * Q_SUB])

        def update(unit, s3):
            c, j, u = unit
            state[j, u] = _softmax_update(s3, vcs[c], *state[j, u])

        _run_units(units, scores, update, UNIT_LOOKAHEAD)
        for j, u in subs:
            cols = slice(u * Q_SUB, (u + 1) * Q_SUB)
            m_sc[j, :, cols], l_sc[j, :, cols], acc_sc[j, :, cols] = state[j, u]
        return carry

    lax.fori_loop(0, seq_len // (KEY_CHUNK * FLASH_CHUNKS), body, 0)

    for j in range(GQA_GROUP):
        o_ref[j * HEAD_DIM:(j + 1) * HEAD_DIM, :] = _normalise(acc_sc[j], l_sc[j]).astype(BF16)


def _full_attention(qt, kt, vt, batch, seq_len):
    tq = ATTN_TILE
    nq = seq_len // tq
    return pl.pallas_call(
        functools.partial(_flash_kernel, seq_len=seq_len),
        grid=(batch, N_KV_HEADS, nq),
        in_specs=[
            pl.BlockSpec((GQA_GROUP * HEAD_DIM, tq), lambda b, g, i: (g, b * nq + i)),
            pl.BlockSpec((HEAD_DIM, seq_len), lambda b, g, i: (g, b)),
            pl.BlockSpec((V_ROWS, seq_len), lambda b, g, i: (g, b)),
        ],
        out_specs=pl.BlockSpec((GQA_GROUP * HEAD_DIM, tq), lambda b, g, i: (g, b * nq + i)),
        out_shape=jax.ShapeDtypeStruct(qt.shape, BF16),
        scratch_shapes=[
            pltpu.VMEM((seq_len, HEAD_DIM), BF16),
            pltpu.VMEM((GQA_GROUP, 8, tq), F32),
            pltpu.VMEM((GQA_GROUP, 8, tq), F32),
            pltpu.VMEM((GQA_GROUP, HEAD_DIM, tq), F32),
        ],
        compiler_params=_params("parallel", "parallel", "arbitrary"),
        name="full_attention",
    )(qt, kt, vt)


C_BLOCKS = 5


def _dilated_bias_table():
    blocks, bases = [], []
    for _, dil in C_GROUPS:
        bases.append(len(blocks))
        blocks += _window_biases(C_RADIUS * dil, dil)
    blocks.append(np.full((KEY_CHUNK, Q_SUB), -np.inf, np.float32))
    return np.stack(blocks), bases


def _dilated_kernel(bias_ref, q_ref, *refs, seq_len, bases):
    k_refs, v_refs, o_ref = refs[:C_BLOCKS], refs[C_BLOCKS:2 * C_BLOCKS], refs[2 * C_BLOCKS]
    q0 = pl.program_id(1) * ATTN_TILE
    centre = (C_BLOCKS // 2) * ATTN_TILE
    kwin = jnp.concatenate([r[...] for r in k_refs], axis=1)
    vwin = jnp.concatenate([r[...] for r in v_refs], axis=1)
    masked_index = bias_ref.shape[0] - 1
    n_sub = ATTN_TILE // Q_SUB

    geom, keys, units, state = {}, {}, [], {}
    for g, (_, dil) in enumerate(C_GROUPS):
        pad, n_chunk = _window_chunks(C_RADIUS * dil)
        geom[g] = (pad, n_chunk)
        assert centre - pad >= 0 and centre + ATTN_TILE + pad <= C_BLOCKS * ATTN_TILE
        keys[g] = kwin[g * HEAD_DIM:(g + 1) * HEAD_DIM, centre - pad:centre + ATTN_TILE + pad].T
        units += [(g, u, c, j) for u in range(n_sub) for c in range(n_chunk) for j in range(GQA_GROUP)]
        for j in range(GQA_GROUP):
            for u in range(n_sub):
                state[g, j, u] = _fresh_state()

    def scores(unit):
        g, u, c, j = unit
        pad, n_chunk = geom[g]
        w0 = u * Q_SUB + c * KEY_CHUNK
        head = g * GQA_GROUP + j
        q = q_ref[head * HEAD_DIM:(head + 1) * HEAD_DIM, u * Q_SUB:(u + 1) * Q_SUB]
        idx = _bias_index(q0 + w0 - pad, c, pad, n_chunk, bases[g], masked_index, seq_len)
        return _scores(keys[g][w0:w0 + KEY_CHUNK], q, bias_ref[idx])

    def update(unit, s3):
        g, u, c, j = unit
        col = centre - geom[g][0] + u * Q_SUB + c * KEY_CHUNK
        vc = vwin[g * V_ROWS:(g + 1) * V_ROWS, col:col + KEY_CHUNK]
        state[g, j, u] = _softmax_update(s3, vc, *state[g, j, u])

    _run_units(units, scores, update, UNIT_LOOKAHEAD)

    groups = range(len(C_GROUPS))
    for j in range(GQA_GROUP):
        for u in range(n_sub):
            mx = functools.reduce(jnp.maximum, [state[g, j, u][0] for g in groups])
            es = [jnp.exp2(state[g, j, u][0] - mx) for g in groups]
            tot = functools.reduce(
                lambda a, b: a + b, [_sublane_all(state[g, j, u][1], jnp.add) * es[g] for g in groups])
            for g in groups:
                r = (g * GQA_GROUP + j) * HEAD_DIM
                o_ref[r:r + HEAD_DIM, u * Q_SUB:(u + 1) * Q_SUB] = _scale_rows(
                    state[g, j, u][2], es[g] / tot).astype(BF16)


def _dilated_attention(qt, kt, vt, batch, seq_len):
    tq = ATTN_TILE
    nq = seq_len // tq
    table, bases = _dilated_bias_table()
    bias = jnp.asarray(table)

    def block(rows, off):
        return pl.BlockSpec((rows, tq), lambda b, i: (0, b * nq + jnp.clip(i + off, 0, nq - 1)))

    offs = range(-(C_BLOCKS // 2), C_BLOCKS // 2 + 1)
    return pl.pallas_call(
        functools.partial(_dilated_kernel, seq_len=seq_len, bases=tuple(bases)),
        grid=(batch, nq),
        in_specs=[pl.BlockSpec(bias.shape, lambda b, i: (0, 0, 0)),
                  pl.BlockSpec((Q_W, tq), lambda b, i: (0, b * nq + i))]
        + [block(KV_W, o) for o in offs] + [block(V_W, o) for o in offs],
        out_specs=pl.BlockSpec((Q_W, tq), lambda b, i: (0, b * nq + i)),
        out_shape=jax.ShapeDtypeStruct(qt.shape, BF16),
        compiler_params=_params("parallel", "parallel"),
        name="dilated_attention",
    )(bias, qt, *([kt] * C_BLOCKS), *([vt] * C_BLOCKS))


def _out_proj_kernel(x_ref, tok_ref, qm_ref, km_ref, vmt_ref, w_ref, g_ref, o_ref):
    tm = x_ref.shape[1]
    mem_out = {}

    def scores(unit):
        h, u = unit
        return _scores(km_ref[h], qm_ref[h * HEAD_DIM:(h + 1) * HEAD_DIM, u * Q_SUB:(u + 1) * Q_SUB])

    def update(unit, s3):
        _, l, acc = _softmax_update(s3, vmt_ref[unit[0]], *_fresh_state())
        mem_out[unit] = _normalise(acc, l).astype(BF16)

    tok_out = {}

    def tok_piece(r, u):
        def piece():
            tok_out[r, u] = jnp.dot(w_ref[r:r + OUT_ROWS, :Q_W], tok_ref[:, u * Q_SUB:(u + 1) * Q_SUB],
                                    preferred_element_type=F32)
        return piece

    n_sub = tm // Q_SUB
    row_blocks = range(0, D_MODEL, OUT_ROWS)
    units = [(h, u) for h in range(N_MEM_HEADS) for u in range(n_sub)]
    _run_units(units, scores, update, UNIT_LOOKAHEAD,
               fillers=[tok_piece(r, u) for u in range(n_sub) for r in row_blocks])
    mo = jnp.concatenate(
        [jnp.concatenate([mem_out[h, u] for u in range(n_sub)], axis=1) for h in range(N_MEM_HEADS)],
        axis=0)
    y_tok = jnp.concatenate(
        [jnp.concatenate([tok_out[r, u] for r in row_blocks], axis=0) for u in range(n_sub)], axis=1)
    y = y_tok + jnp.dot(w_ref[:, Q_W:], mo, preferred_element_type=F32)
    o_ref[...] = x_ref[...] + _rms_rows(y, g_ref[...])


def _out_proj(xt, tok_t, qm_t, km, vmt, wt, g_col, seq_len):
    t = xt.shape[1]
    tm = OUT_TILE
    per_seq = seq_len // tm
    return pl.pallas_call(
        _out_proj_kernel,
        grid=(t // tm,),
        in_specs=[
            pl.BlockSpec((D_MODEL, tm), lambda i: (0, i)),
            pl.BlockSpec((Q_W, tm), lambda i: (0, i)),
            pl.BlockSpec((QM_W, tm), lambda i: (0, i)),
            pl.BlockSpec((None, N_MEM_HEADS, N_MEM, HEAD_DIM), lambda i: (i // per_seq, 0, 0, 0)),
            pl.BlockSpec((None, N_MEM_HEADS, V_ROWS, N_MEM), lambda i: (i // per_seq, 0, 0, 0)),
            pl.BlockSpec((D_MODEL, MIX_WIDTH), lambda i: (0, 0)),
            pl.BlockSpec((D_MODEL, 1), lambda i: (0, 0)),
        ],
        out_specs=pl.BlockSpec((D_MODEL, tm), lambda i: (0, i)),
        out_shape=jax.ShapeDtypeStruct(xt.shape, F32),
        compiler_params=_params("parallel"),
        name="out_proj",
    )(xt, tok_t, qm_t, km, vmt, wt, g_col)


def _ffn_kernel(x_ref, gpre_ref, wgu_ref, wd_ref, gpost_ref, o_ref, *, token_major_out):
    h = _rms_rows(x_ref[...], gpre_ref[...]).astype(BF16)
    n_blocks = D_FF // FF_BLOCK
    gus, y = {}, None
    for i in range(n_blocks + 1):
        if i < n_blocks:
            gus[i] = jnp.dot(wgu_ref[2 * i * FF_BLOCK:2 * (i + 1) * FF_BLOCK, :], h, preferred_element_type=F32)
        if i >= 1:
            gu = gus.pop(i - 1)
            gate, up = gu[:FF_BLOCK], gu[FF_BLOCK:]
            act = (gate * jax.nn.sigmoid(gate) * up).astype(BF16)
            part = jnp.dot(wd_ref[:, (i - 1) * FF_BLOCK:i * FF_BLOCK], act, preferred_element_type=F32)
            y = part if y is None else y + part
    out = x_ref[...] + _rms_rows(y, gpost_ref[...])
    o_ref[...] = out.T if token_major_out else out


def _ffn(xt, gpre, wgu_t, wd_t, gpost, token_major_out=False):
    d, t = xt.shape
    tm = TOKEN_TILE
    resident = pl.Buffered(1)
    if token_major_out:
        out_spec, out_shape = pl.BlockSpec((tm, d), lambda i: (i, 0)), (t, d)
    else:
        out_spec, out_shape = pl.BlockSpec((d, tm), lambda i: (0, i)), (d, t)
    return pl.pallas_call(
        functools.partial(_ffn_kernel, token_major_out=token_major_out),
        grid=(t // tm,),
        in_specs=[
            pl.BlockSpec((d, tm), lambda i: (0, i)),
            pl.BlockSpec((d, 1), lambda i: (0, 0)),
            pl.BlockSpec((2 * D_FF, d), lambda i: (0, 0), pipeline_mode=resident),
            pl.BlockSpec((d, D_FF), lambda i: (0, 0), pipeline_mode=resident),
            pl.BlockSpec((d, 1), lambda i: (0, 0)),
        ],
        out_specs=out_spec,
        out_shape=jax.ShapeDtypeStruct(out_shape, F32),
        compiler_params=_params("parallel"),
        name="ffn",
    )(xt, gpre, wgu_t, wd_t, gpost)


def _rope_table_t(pos, n_dims, theta):
    inv = theta ** (-(jnp.arange(0, n_dims, 2, dtype=F32) / n_dims))
    ang = inv[:, None] * pos.astype(F32)[None, :]
    return jnp.cos(ang), jnp.sin(ang)


def kernel(x, mem, mem_norm_g, w_in, w_mem_kv, w_o, g_mix_pre, g_mix_post, attn_sink, qk_norm_g,
           w_gate_up, w_down, g_ffn_pre, g_ffn_post):
    batch, seq_len, _ = x.shape
    assert seq_len % (KEY_CHUNK * FLASH_CHUNKS) == 0 and seq_len % OUT_TILE == 0
    pos = jnp.arange(seq_len, dtype=jnp.int32)
    partial_tables = _rope_table_t(pos, ROPE_DIMS, ROPE_THETA)
    axial_tables = (_rope_table_t(pos // GRID_W, HEAD_DIM // 2, AXIAL_THETA)
                    + _rope_table_t(pos % GRID_W, HEAD_DIM // 2, AXIAL_THETA))

    w_in_t = jnp.swapaxes(w_in, 1, 2).astype(BF16)
    w_o_t = jnp.swapaxes(w_o, 1, 2).astype(BF16)
    w_gu_t = jnp.swapaxes(w_gate_up, 1, 2).astype(BF16)
    w_gu_t = jnp.swapaxes(w_gu_t.reshape(DEPTH, 2, D_FF // FF_BLOCK, FF_BLOCK, D_MODEL), 1, 2).reshape(
        DEPTH, 2 * D_FF, D_MODEL)
    w_d_t = jnp.swapaxes(w_down, 1, 2).astype(BF16)
    w_mem = w_mem_kv.astype(BF16).reshape(DEPTH, D_MODEL, 2, N_MEM_HEADS, HEAD_DIM)
    w_mk = jnp.transpose(w_mem[:, :, 0], (0, 2, 1, 3))
    w_mv_t = jnp.transpose(w_mem[:, :, 1], (0, 2, 3, 1))

    km, vmt = _mem_kv(mem, mem_norm_g.reshape(1, D_MODEL), w_mk, w_mv_t)

    xt = None
    for i in range(DEPTH):
        kind = i % N_MIXERS
        g_pre = g_mix_pre[i].reshape(D_MODEL, 1)
        if kind == 1:
            qk_g = qk_norm_g[i // N_MIXERS].reshape(2, HEAD_DIM, 1)
            qt, kt, vt, qmt = _in_proj(xt, g_pre, w_in_t[i], axial_tables, qk_g, seq_len)
            tok_t = _full_attention(qt, kt, vt, batch, seq_len)
        else:
            if i == 0:
                xt, qt, kt, vt, qmt = _in_proj(x.reshape(batch * seq_len, D_MODEL), g_pre.reshape(1, D_MODEL),
                                               w_in_t[i], partial_tables, None, seq_len, entry=True)
            else:
                qt, kt, vt, qmt = _in_proj(xt, g_pre, w_in_t[i], partial_tables, None, seq_len)
            if kind == 0:
                tok_t = _banded_attention(attn_sink[i // N_MIXERS], qt, kt, vt, batch, seq_len)
            else:
                tok_t = _dilated_attention(qt, kt, vt, batch, seq_len)
        xt = _out_proj(xt, tok_t, qmt, km[i], vmt[i], w_o_t[i], g_mix_post[i].reshape(D_MODEL, 1), seq_len)
        xt = _ffn(xt, g_ffn_pre[i].reshape(D_MODEL, 1), w_gu_t[i], w_d_t[i], g_ffn_post[i].reshape(D_MODEL, 1),
                  token_major_out=(i == DEPTH - 1))
    return xt.reshape(batch, seq_len, D_MODEL)
```

```python
import functools

import numpy as np
import jax
import jax.numpy as jnp
from jax import lax
from jax.experimental import pallas as pl
from jax.experimental.pallas import tpu as pltpu

D_MODEL = 1024
DEPTH = 4
HEAD_DIM = 64
N_TOK_HEADS = 12
N_KV_HEADS = 3
GQA_GROUP = N_TOK_HEADS // N_KV_HEADS
N_MEM_HEADS = 4
N_MEM = 256
Q_W = N_TOK_HEADS * HEAD_DIM
KV_W = N_KV_HEADS * HEAD_DIM
QM_W = N_MEM_HEADS * HEAD_DIM
IN_W = Q_W + 2 * KV_W + QM_W
MIX_WIDTH = Q_W + QM_W
D_FF = -(-8 * D_MODEL // (3 * 256)) * 256
N_MIXERS = 3
A_RADIUS = 128
C_GROUPS = ((128, 1), (512, 4), (2048, 16))
C_RADIUS = 64
ROPE_THETA = 500000.0
ROPE_DIMS = HEAD_DIM // 4
AXIAL_THETA = 10000.0
GRID_W = 64
EPS = 1e-6
ATTN_SCALE = HEAD_DIM ** -0.5
LOG2E = 1.4426950408889634
Q_SCALE = ATTN_SCALE * LOG2E
NEG_INIT = -1e30

F32 = jnp.float32
BF16 = jnp.bfloat16
BF16_ROWS = 16

V_ROWS = HEAD_DIM + BF16_ROWS
V_W = N_KV_HEADS * V_ROWS

TOKEN_TILE = 512
OUT_TILE = 1024
OUT_ROWS = 256
FF_BLOCK = 256
ATTN_TILE = 512
Q_SUB = 256
KEY_CHUNK = 256
FLASH_CHUNKS = 32
UNIT_LOOKAHEAD = 5
VMEM_LIMIT = 56 * 1024 * 1024

_NT = (((1,), (1,)), ((), ()))


def _params(*semantics):
    return pltpu.CompilerParams(dimension_semantics=semantics, vmem_limit_bytes=VMEM_LIMIT)


def _rms_rows(x, g_col):
    ms = jnp.mean(x * x, axis=0, keepdims=True)
    return x * lax.rsqrt(ms + EPS) * g_col


def _ones_tile(n):
    return jnp.where(lax.broadcasted_iota(jnp.int32, (BF16_ROWS, n), 0) == 0, 1.0, 0.0).astype(BF16)


def _mem_kv_kernel(mem_ref, g_ref, wk_ref, wvt_ref, km_ref, vmt_ref):
    m = mem_ref[...]
    ms = jnp.mean(m * m, axis=-1, keepdims=True)
    mn = (m * lax.rsqrt(ms + EPS) * g_ref[...]).astype(BF16)
    for h in range(N_MEM_HEADS):
        km_ref[h] = jnp.dot(mn, wk_ref[h], preferred_element_type=F32).astype(BF16)
        vmt_ref[h, :HEAD_DIM] = lax.dot_general(wvt_ref[h], mn, _NT, preferred_element_type=F32).astype(BF16)
        vmt_ref[h, HEAD_DIM:] = _ones_tile(N_MEM)


def _mem_kv(mem, g_row, wk, wvt):
    b = mem.shape[0]
    return pl.pallas_call(
        _mem_kv_kernel,
        grid=(DEPTH, b),
        in_specs=[
            pl.BlockSpec((None, N_MEM, D_MODEL), lambda l, i: (i, 0, 0)),
            pl.BlockSpec((1, D_MODEL), lambda l, i: (0, 0)),
            pl.BlockSpec((None, N_MEM_HEADS, D_MODEL, HEAD_DIM), lambda l, i: (l, 0, 0, 0)),
            pl.BlockSpec((None, N_MEM_HEADS, HEAD_DIM, D_MODEL), lambda l, i: (l, 0, 0, 0)),
        ],
        out_specs=[
            pl.BlockSpec((None, None, N_MEM_HEADS, N_MEM, HEAD_DIM), lambda l, i: (l, i, 0, 0, 0)),
            pl.BlockSpec((None, None, N_MEM_HEADS, V_ROWS, N_MEM), lambda l, i: (l, i, 0, 0, 0)),
        ],
        out_shape=[
            jax.ShapeDtypeStruct((DEPTH, b, N_MEM_HEADS, N_MEM, HEAD_DIM), BF16),
            jax.ShapeDtypeStruct((DEPTH, b, N_MEM_HEADS, V_ROWS, N_MEM), BF16),
        ],
        compiler_params=_params("parallel", "parallel"),
        name="mem_kv",
    )(mem, g_row, wk, wvt)


def _split_store(p, rot, q_ref, k_ref, v_ref, qm_ref):
    for hd in range(N_TOK_HEADS):
        r = hd * HEAD_DIM
        q_ref[r:r + HEAD_DIM, :] = (rot(p[r:r + HEAD_DIM], False) * Q_SCALE).astype(BF16)
    ones = _ones_tile(p.shape[1])
    for hd in range(N_KV_HEADS):
        r = hd * HEAD_DIM
        k_ref[r:r + HEAD_DIM, :] = rot(p[Q_W + r:Q_W + r + HEAD_DIM], True).astype(BF16)
        v_ref[hd * V_ROWS:hd * V_ROWS + HEAD_DIM, :] = p[Q_W + KV_W + r:Q_W + KV_W + r + HEAD_DIM].astype(BF16)
        v_ref[hd * V_ROWS + HEAD_DIM:(hd + 1) * V_ROWS, :] = ones
    qm_ref[...] = (p[Q_W + 2 * KV_W:] * Q_SCALE).astype(BF16)


def _partial_rope_store(p, cos_ref, sin_ref, q_ref, k_ref, v_ref, qm_ref):
    c, s = cos_ref[...], sin_ref[...]
    half = ROPE_DIMS // 2

    def rot(head, is_key):
        x1, x2 = head[:half], head[half:ROPE_DIMS]
        return jnp.concatenate([x1 * c - x2 * s, x2 * c + x1 * s, head[ROPE_DIMS:]], axis=0)

    _split_store(p, rot, q_ref, k_ref, v_ref, qm_ref)


def _in_proj_partial_kernel(x_ref, g_ref, w_ref, cos_ref, sin_ref, q_ref, k_ref, v_ref, qm_ref):
    h = _rms_rows(x_ref[...], g_ref[...]).astype(BF16)
    p = jnp.dot(w_ref[...], h, preferred_element_type=F32)
    _partial_rope_store(p, cos_ref, sin_ref, q_ref, k_ref, v_ref, qm_ref)


def _in_proj_entry_kernel(x_ref, g_ref, w_ref, cos_ref, sin_ref, xt_ref, q_ref, k_ref, v_ref, qm_ref):
    x = x_ref[...]
    ms = jnp.mean(x * x, axis=-1, keepdims=True)
    h = (x * lax.rsqrt(ms + EPS) * g_ref[...]).astype(BF16)
    p = lax.dot_general(w_ref[...], h, _NT, preferred_element_type=F32)
    xt_ref[...] = x.T
    _partial_rope_store(p, cos_ref, sin_ref, q_ref, k_ref, v_ref, qm_ref)


def _in_proj_axial_kernel(x_ref, g_ref, w_ref, cr_ref, sr_ref, cc_ref, sc_ref, qkg_ref,
                          q_ref, k_ref, v_ref, qm_ref):
    h = _rms_rows(x_ref[...], g_ref[...]).astype(BF16)
    p = jnp.dot(w_ref[...], h, preferred_element_type=F32)
    cr, sr, cc, sc = cr_ref[...], sr_ref[...], cc_ref[...], sc_ref[...]
    qr = HEAD_DIM // 4

    def rot(head, is_key):
        y = _rms_rows(head, qkg_ref[1] if is_key else qkg_ref[0])
        a1, a2, b1, b2 = y[:qr], y[qr:2 * qr], y[2 * qr:3 * qr], y[3 * qr:]
        return jnp.concatenate(
            [a1 * cr - a2 * sr, a2 * cr + a1 * sr, b1 * cc - b2 * sc, b2 * cc + b1 * sc], axis=0)

    _split_store(p, rot, q_ref, k_ref, v_ref, qm_ref)


def _in_proj(x, g, wt, tables, qk_g, seq_len, entry=False):
    tm = TOKEN_TILE
    t = x.shape[0] if entry else x.shape[1]
    n_seq_tiles = seq_len // tm
    tab_specs = [pl.BlockSpec((tb.shape[0], tm), lambda i: (0, i % n_seq_tiles)) for tb in tables]
    x_spec = pl.BlockSpec((tm, D_MODEL), lambda i: (i, 0)) if entry else pl.BlockSpec((D_MODEL, tm), lambda i: (0, i))
    in_specs = [x_spec, pl.BlockSpec(g.shape, lambda i: (0, 0)), pl.BlockSpec((IN_W, D_MODEL), lambda i: (0, 0))]
    in_specs += tab_specs
    args = [x, g, wt] + list(tables)
    widths, dtypes = (Q_W, KV_W, V_W, QM_W), (BF16,) * 4
    if entry:
        assert qk_g is None
        body = _in_proj_entry_kernel
        widths, dtypes = (D_MODEL,) + widths, (F32,) + dtypes
    elif qk_g is None:
        body = _in_proj_partial_kernel
    else:
        body = _in_proj_axial_kernel
        in_specs.append(pl.BlockSpec((2, HEAD_DIM, 1), lambda i: (0, 0, 0)))
        args.append(qk_g)
    return pl.pallas_call(
        body,
        grid=(t // tm,),
        in_specs=in_specs,
        out_specs=[pl.BlockSpec((w, tm), lambda i: (0, i)) for w in widths],
        out_shape=[jax.ShapeDtypeStruct((w, t), dt) for w, dt in zip(widths, dtypes)],
        compiler_params=_params("parallel"),
        name="in_proj",
    )(*args)


def _sublane_all(x, op):
    for shift in (4, 2, 1):
        x = op(x, pltpu.roll(x, shift, 0))
    return x


def _scores(kc, q, bias=None):
    s = jnp.dot(kc, q, preferred_element_type=F32)
    if bias is not None:
        s = s + bias
    return s.reshape(kc.shape[0] // 8, 8, q.shape[1])


def _softmax_update(s3, vc, m_old, l_old, acc_old):
    n, _, qs = s3.shape
    m_new = jnp.maximum(m_old, _sublane_all(jnp.max(s3, axis=0), jnp.maximum))
    alpha = jnp.exp2(m_old - m_new)
    p = jnp.exp2((s3 - m_new[None]).reshape(n * 8, qs).astype(BF16))
    pv = jnp.dot(vc, p, preferred_element_type=F32)
    l_new = alpha * l_old + pv[HEAD_DIM:HEAD_DIM + 8]
    acc3 = alpha[None] * acc_old.reshape(HEAD_DIM // 8, 8, qs) + pv[:HEAD_DIM].reshape(HEAD_DIM // 8, 8, qs)
    return m_new, l_new, acc3.reshape(HEAD_DIM, qs)


def _fresh_state():
    return (jnp.full((8, Q_SUB), NEG_INIT, F32), jnp.zeros((8, Q_SUB), F32),
            jnp.zeros((HEAD_DIM, Q_SUB), F32))


def _run_units(units, score_fn, update_fn, lookahead, fillers=()):
    pending, fillers = {}, list(fillers)
    for n in range(len(units) + lookahead):
        if n < len(units):
            pending[n] = score_fn(units[n])
        if fillers and n >= lookahead:
            fillers.pop(0)()
        if n >= lookahead:
            update_fn(units[n - lookahead], pending.pop(n - lookahead))
    for f in fillers:
        f()


def _scale_rows(acc, w):
    qs = acc.shape[1]
    return (acc.reshape(HEAD_DIM // 8, 8, qs) * w[None]).reshape(HEAD_DIM, qs)


def _normalise(acc, l):
    return _scale_rows(acc, 1.0 / _sublane_all(l, jnp.add))


def _band_bias(delta0, reach, dil, row_lo=0, row_hi=KEY_CHUNK):
    kr = np.arange(KEY_CHUNK)[:, None]
    delta = delta0 + kr - np.arange(Q_SUB)[None, :]
    ok = (np.abs(delta) <= reach) & (delta % dil == 0) & (kr >= row_lo) & (kr < row_hi)
    return np.where(ok, 0.0, -np.inf).astype(np.float32)


def _window_chunks(reach):
    pad = -(-reach // 128) * 128
    return pad, (Q_SUB + 2 * pad) // KEY_CHUNK


def _window_biases(reach, dil):
    pad, n_chunk = _window_chunks(reach)
    out = [_band_bias(c * KEY_CHUNK - pad, reach, dil) for c in range(n_chunk)]
    if pad % KEY_CHUNK:
        out.append(_band_bias(-pad, reach, dil, row_lo=pad % KEY_CHUNK))
        out.append(_band_bias((n_chunk - 1) * KEY_CHUNK - pad, reach, dil, row_hi=KEY_CHUNK - pad % KEY_CHUNK))
    return out


def _bias_index(start, c, pad, n_chunk, base, masked_index, seq_len):
    if pad % KEY_CHUNK == 0:
        on = (start >= 0) & (start + KEY_CHUNK <= seq_len)
        return jnp.where(on, base + c, masked_index)
    idx = base + c
    if c == 0:
        idx = jnp.where(start < 0, base + n_chunk, idx)
    if c == n_chunk - 1:
        idx = jnp.where(start + KEY_CHUNK > seq_len, base + n_chunk + 1, idx)
    return idx


def _banded_kernel(sink_ref, bias_ref, q_ref, kl_ref, km_ref, kr_ref, vl_ref, vm_ref, vr_ref, o_ref, *,
                   seq_len):
    g = pl.program_id(1)
    q0 = pl.program_id(2) * ATTN_TILE
    pad, n_chunk = _window_chunks(A_RADIUS)
    kwin = jnp.concatenate([kl_ref[...], km_ref[...], kr_ref[...]], axis=1).T
    vwin = jnp.concatenate([vl_ref[...], vm_ref[...], vr_ref[...]], axis=1)
    n_sub = ATTN_TILE // Q_SUB

    first_sublane = lax.broadcasted_iota(jnp.int32, (8, Q_SUB), 0) == 0
    state = {}
    for j in range(GQA_GROUP):
        sink = sink_ref[g * GQA_GROUP + j] * LOG2E
        for u in range(n_sub):
            state[j, u] = (jnp.full((8, Q_SUB), sink, F32), jnp.where(first_sublane, 1.0, 0.0),
                           jnp.zeros((HEAD_DIM, Q_SUB), F32))

    def scores(unit):
        u, c, j = unit
        w0 = u * Q_SUB + c * KEY_CHUNK
        q = q_ref[j * HEAD_DIM:(j + 1) * HEAD_DIM, u * Q_SUB:(u + 1) * Q_SUB]
        idx = _bias_index(q0 + w0 - pad, c, pad, n_chunk, 0, None, seq_len)
        return _scores(kwin[w0:w0 + KEY_CHUNK], q, bias_ref[idx])

    def update(unit, s3):
        u, c, j = unit
        w0 = u * Q_SUB + c * KEY_CHUNK
        state[j, u] = _softmax_update(s3, vwin[:, w0:w0 + KEY_CHUNK], *state[j, u])

    units = [(u, c, j) for u in range(n_sub) for c in range(n_chunk) for j in range(GQA_GROUP)]
    _run_units(units, scores, update, UNIT_LOOKAHEAD)

    for (j, u), (_, l, acc) in state.items():
        o_ref[j * HEAD_DIM:(j + 1) * HEAD_DIM, u * Q_SUB:(u + 1) * Q_SUB] = _normalise(acc, l).astype(BF16)


def _banded_attention(sink, qt, kt, vt, batch, seq_len):
    tq = ATTN_TILE
    nq = seq_len // tq
    pad, _ = _window_chunks(A_RADIUS)
    halo_per_tile = tq // pad
    n_halo = seq_len // pad
    bias = jnp.asarray(np.stack(_window_biases(A_RADIUS, 1)))

    def left(b, g, i):
        return (g, b * n_halo + jnp.maximum(i * halo_per_tile - 1, 0))

    def right(b, g, i):
        return (g, b * n_halo + jnp.minimum((i + 1) * halo_per_tile, n_halo - 1))

    def main(b, g, i):
        return (g, b * nq + i)

    def kv_specs(rows):
        return [pl.BlockSpec((rows, pad), left), pl.BlockSpec((rows, tq), main), pl.BlockSpec((rows, pad), right)]

    return pl.pallas_call(
        functools.partial(_banded_kernel, seq_len=seq_len),
        grid=(batch, N_KV_HEADS, nq),
        in_specs=[
            pl.BlockSpec(memory_space=pltpu.SMEM),
            pl.BlockSpec(bias.shape, lambda b, g, i: (0, 0, 0)),
            pl.BlockSpec((GQA_GROUP * HEAD_DIM, tq), main),
        ] + kv_specs(HEAD_DIM) + kv_specs(V_ROWS),
        out_specs=pl.BlockSpec((GQA_GROUP * HEAD_DIM, tq), main),
        out_shape=jax.ShapeDtypeStruct(qt.shape, BF16),
        compiler_params=_params("parallel", "parallel", "parallel"),
        name="banded_attention",
    )(sink, bias, qt, kt, kt, kt, vt, vt, vt)


def _flash_kernel(q_ref, k_ref, v_ref, o_ref, k_sc, m_sc, l_sc, acc_sc, *, seq_len):
    tq = q_ref.shape[1]

    @pl.when(pl.program_id(2) == 0)
    def _():
        def fill(c, carry):
            off = pl.multiple_of(c * ATTN_TILE, ATTN_TILE)
            k_sc[pl.ds(off, ATTN_TILE), :] = k_ref[:, pl.ds(off, ATTN_TILE)].T
            return carry
        lax.fori_loop(0, seq_len // ATTN_TILE, fill, 0)

    m_sc[...] = jnp.full(m_sc.shape, NEG_INIT, F32)
    l_sc[...] = jnp.zeros(l_sc.shape, F32)
    acc_sc[...] = jnp.zeros(acc_sc.shape, F32)

    subs = [(j, u) for j in range(GQA_GROUP) for u in range(tq // Q_SUB)]
    units = [(c, j, u) for c in range(FLASH_CHUNKS) for (j, u) in subs]

    def body(it, carry):
        kcs, vcs = [], []
        for c in range(FLASH_CHUNKS):
            off = pl.multiple_of((it * FLASH_CHUNKS + c) * KEY_CHUNK, KEY_CHUNK)
            kcs.append(k_sc[pl.ds(off, KEY_CHUNK), :])
            vcs.append(v_ref[:, pl.ds(off, KEY_CHUNK)])
        state = {}
        for j, u in subs:
            cols = slice(u * Q_SUB, (u + 1) * Q_SUB)
            state[j, u] = (m_sc[j, :, cols], l_sc[j, :, cols], acc_sc[j, :, cols])

        def scores(unit):
            c, j, u = unit
            return _scores(kcs[c], q_ref[j * HEAD_DIM:(j + 1) * HEAD_DIM, u * Q_SUB:(u + 1) * Q_SUB])

        def update(unit, s3):
            c, j, u = unit
            state[j, u] = _softmax_update(s3, vcs[c], *state[j, u])

        _run_units(units, scores, update, UNIT_LOOKAHEAD)
        for j, u in subs:
            cols = slice(u * Q_SUB, (u + 1) * Q_SUB)
            m_sc[j, :, cols], l_sc[j, :, cols], acc_sc[j, :, cols] = state[j, u]
        return carry

    lax.fori_loop(0, seq_len // (KEY_CHUNK * FLASH_CHUNKS), body, 0)

    for j in range(GQA_GROUP):
        o_ref[j * HEAD_DIM:(j + 1) * HEAD_DIM, :] = _normalise(acc_sc[j], l_sc[j]).astype(BF16)


def _full_attention(qt, kt, vt, batch, seq_len):
    tq = ATTN_TILE
    nq = seq_len // tq
    return pl.pallas_call(
        functools.partial(_flash_kernel, seq_len=seq_len),
        grid=(batch, N_KV_HEADS, nq),
        in_specs=[
            pl.BlockSpec((GQA_GROUP * HEAD_DIM, tq), lambda b, g, i: (g, b * nq + i)),
            pl.BlockSpec((HEAD_DIM, seq_len), lambda b, g, i: (g, b)),
            pl.BlockSpec((V_ROWS, seq_len), lambda b, g, i: (g, b)),
        ],
        out_specs=pl.BlockSpec((GQA_GROUP * HEAD_DIM, tq), lambda b, g, i: (g, b * nq + i)),
        out_shape=jax.ShapeDtypeStruct(qt.shape, BF16),
        scratch_shapes=[
            pltpu.VMEM((seq_len, HEAD_DIM), BF16),
            pltpu.VMEM((GQA_GROUP, 8, tq), F32),
            pltpu.VMEM((GQA_GROUP, 8, tq), F32),
            pltpu.VMEM((GQA_GROUP, HEAD_DIM, tq), F32),
        ],
        compiler_params=_params("parallel", "parallel", "arbitrary"),
        name="full_attention",
    )(qt, kt, vt)


C_BLOCKS = 5


def _dilated_bias_table():
    blocks, bases = [], []
    for _, dil in C_GROUPS:
        bases.append(len(blocks))
        blocks += _window_biases(C_RADIUS * dil, dil)
    blocks.append(np.full((KEY_CHUNK, Q_SUB), -np.inf, np.float32))
    return np.stack(blocks), bases


def _dilated_kernel(bias_ref, q_ref, *refs, seq_len, bases):
    k_refs, v_refs, o_ref = refs[:C_BLOCKS], refs[C_BLOCKS:2 * C_BLOCKS], refs[2 * C_BLOCKS]
    q0 = pl.program_id(1) * ATTN_TILE
    centre = (C_BLOCKS // 2) * ATTN_TILE
    kwin = jnp.concatenate([r[...] for r in k_refs], axis=1)
    vwin = jnp.concatenate([r[...] for r in v_refs], axis=1)
    masked_index = bias_ref.shape[0] - 1
    n_sub = ATTN_TILE // Q_SUB

    geom, keys, units, state = {}, {}, [], {}
    for g, (_, dil) in enumerate(C_GROUPS):
        pad, n_chunk = _window_chunks(C_RADIUS * dil)
        geom[g] = (pad, n_chunk)
        assert centre - pad >= 0 and centre + ATTN_TILE + pad <= C_BLOCKS * ATTN_TILE
        keys[g] = kwin[g * HEAD_DIM:(g + 1) * HEAD_DIM, centre - pad:centre + ATTN_TILE + pad].T
        units += [(g, u, c, j) for u in range(n_sub) for c in range(n_chunk) for j in range(GQA_GROUP)]
        for j in range(GQA_GROUP):
            for u in range(n_sub):
                state[g, j, u] = _fresh_state()

    def scores(unit):
        g, u, c, j = unit
        pad, n_chunk = geom[g]
        w0 = u * Q_SUB + c * KEY_CHUNK
        head = g * GQA_GROUP + j
        q = q_ref[head * HEAD_DIM:(head + 1) * HEAD_DIM, u * Q_SUB:(u + 1) * Q_SUB]
        idx = _bias_index(q0 + w0 - pad, c, pad, n_chunk, bases[g], masked_index, seq_len)
        return _scores(keys[g][w0:w0 + KEY_CHUNK], q, bias_ref[idx])

    def update(unit, s3):
        g, u, c, j = unit
        col = centre - geom[g][0] + u * Q_SUB + c * KEY_CHUNK
        vc = vwin[g * V_ROWS:(g + 1) * V_ROWS, col:col + KEY_CHUNK]
        state[g, j, u] = _softmax_update(s3, vc, *state[g, j, u])

    _run_units(units, scores, update, UNIT_LOOKAHEAD)

    groups = range(len(C_GROUPS))
    for j in range(GQA_GROUP):
        for u in range(n_sub):
            mx = functools.reduce(jnp.maximum, [state[g, j, u][0] for g in groups])
            es = [jnp.exp2(state[g, j, u][0] - mx) for g in groups]
            tot = functools.reduce(
                lambda a, b: a + b, [_sublane_all(state[g, j, u][1], jnp.add) * es[g] for g in groups])
            for g in groups:
                r = (g * GQA_GROUP + j) * HEAD_DIM
                o_ref[r:r + HEAD_DIM, u * Q_SUB:(u + 1) * Q_SUB] = _scale_rows(
                    state[g, j, u][2], es[g] / tot).astype(BF16)


def _dilated_attention(qt, kt, vt, batch, seq_len):
    tq = ATTN_TILE
    nq = seq_len // tq
    table, bases = _dilated_bias_table()
    bias = jnp.asarray(table)

    def block(rows, off):
        return pl.BlockSpec((rows, tq), lambda b, i: (0, b * nq + jnp.clip(i + off, 0, nq - 1)))

    offs = range(-(C_BLOCKS // 2), C_BLOCKS // 2 + 1)
    return pl.pallas_call(
        functools.partial(_dilated_kernel, seq_len=seq_len, bases=tuple(bases)),
        grid=(batch, nq),
        in_specs=[pl.BlockSpec(bias.shape, lambda b, i: (0, 0, 0)),
                  pl.BlockSpec((Q_W, tq), lambda b, i: (0, b * nq + i))]
        + [block(KV_W, o) for o in offs] + [block(V_W, o) for o in offs],
        out_specs=pl.BlockSpec((Q_W, tq), lambda b, i: (0, b * nq + i)),
        out_shape=jax.ShapeDtypeStruct(qt.shape, BF16),
        compiler_params=_params("parallel", "parallel"),
        name="dilated_attention",
    )(bias, qt, *([kt] * C_BLOCKS), *([vt] * C_BLOCKS))


def _out_proj_kernel(x_ref, tok_ref, qm_ref, km_ref, vmt_ref, w_ref, g_ref, o_ref):
    tm = x_ref.shape[1]
    mem_out = {}

    def scores(unit):
        h, u = unit
        return _scores(km_ref[h], qm_ref[h * HEAD_DIM:(h + 1) * HEAD_DIM, u * Q_SUB:(u + 1) * Q_SUB])

    def update(unit, s3):
        _, l, acc = _softmax_update(s3, vmt_ref[unit[0]], *_fresh_state())
        mem_out[unit] = _normalise(acc, l).astype(BF16)

    tok_out = {}

    def tok_piece(r, u):
        def piece():
            tok_out[r, u] = jnp.dot(w_ref[r:r + OUT_ROWS, :Q_W], tok_ref[:, u * Q_SUB:(u + 1) * Q_SUB],
                                    preferred_element_type=F32)
        return piece

    n_sub = tm // Q_SUB
    row_blocks = range(0, D_MODEL, OUT_ROWS)
    units = [(h, u) for h in range(N_MEM_HEADS) for u in range(n_sub)]
    _run_units(units, scores, update, UNIT_LOOKAHEAD,
               fillers=[tok_piece(r, u) for u in range(n_sub) for r in row_blocks])
    mo = jnp.concatenate(
        [jnp.concatenate([mem_out[h, u] for u in range(n_sub)], axis=1) for h in range(N_MEM_HEADS)],
        axis=0)
    y_tok = jnp.concatenate(
        [jnp.concatenate([tok_out[r, u] for r in row_blocks], axis=0) for u in range(n_sub)], axis=1)
    y = y_tok + jnp.dot(w_ref[:, Q_W:], mo, preferred_element_type=F32)
    o_ref[...] = x_ref[...] + _rms_rows(y, g_ref[...])


def _out_proj(xt, tok_t, qm_t, km, vmt, wt, g_col, seq_len):
    t = xt.shape[1]
    tm = OUT_TILE
    per_seq = seq_len // tm
    return pl.pallas_call(
        _out_proj_kernel,
        grid=(t // tm,),
        in_specs=[
            pl.BlockSpec((D_MODEL, tm), lambda i: (0, i)),
            pl.BlockSpec((Q_W, tm), lambda i: (0, i)),
            pl.BlockSpec((QM_W, tm), lambda i: (0, i)),
            pl.BlockSpec((None, N_MEM_HEADS, N_MEM, HEAD_DIM), lambda i: (i // per_seq, 0, 0, 0)),
            pl.BlockSpec((None, N_MEM_HEADS, V_ROWS, N_MEM), lambda i: (i // per_seq, 0, 0, 0)),
            pl.BlockSpec((D_MODEL, MIX_WIDTH), lambda i: (0, 0)),
            pl.BlockSpec((D_MODEL, 1), lambda i: (0, 0)),
        ],
        out_specs=pl.BlockSpec((D_MODEL, tm), lambda i: (0, i)),
        out_shape=jax.ShapeDtypeStruct(xt.shape, F32),
        compiler_params=_params("parallel"),
        name="out_proj",
    )(xt, tok_t, qm_t, km, vmt, wt, g_col)


def _ffn_kernel(x_ref, gpre_ref, wgu_ref, wd_ref, gpost_ref, o_ref, *, token_major_out):
    h = _rms_rows(x_ref[...], gpre_ref[...]).astype(BF16)
    n_blocks = D_FF // FF_BLOCK
    gus, y = {}, None
    for i in range(n_blocks + 1):
        if i < n_blocks:
            gus[i] = jnp.dot(wgu_ref[2 * i * FF_BLOCK:2 * (i + 1) * FF_BLOCK, :], h, preferred_element_type=F32)
        if i >= 1:
            gu = gus.pop(i - 1)
            gate, up = gu[:FF_BLOCK], gu[FF_BLOCK:]
            act = (gate * jax.nn.sigmoid(gate) * up).astype(BF16)
            part = jnp.dot(wd_ref[:, (i - 1) * FF_BLOCK:i * FF_BLOCK], act, preferred_element_type=F32)
            y = part if y is None else y + part
    out = x_ref[...] + _rms_rows(y, gpost_ref[...])
    o_ref[...] = out.T if token_major_out else out


def _ffn(xt, gpre, wgu_t, wd_t, gpost, token_major_out=False):
    d, t = xt.shape
    tm = TOKEN_TILE
    resident = pl.Buffered(1)
    if token_major_out:
        out_spec, out_shape = pl.BlockSpec((tm, d), lambda i: (i, 0)), (t, d)
    else:
        out_spec, out_shape = pl.BlockSpec((d, tm), lambda i: (0, i)), (d, t)
    return pl.pallas_call(
        functools.partial(_ffn_kernel, token_major_out=token_major_out),
        grid=(t // tm,),
        in_specs=[
            pl.BlockSpec((d, tm), lambda i: (0, i)),
            pl.BlockSpec((d, 1), lambda i: (0, 0)),
            pl.BlockSpec((2 * D_FF, d), lambda i: (0, 0), pipeline_mode=resident),
            pl.BlockSpec((d, D_FF), lambda i: (0, 0), pipeline_mode=resident),
            pl.BlockSpec((d, 1), lambda i: (0, 0)),
        ],
        out_specs=out_spec,
        out_shape=jax.ShapeDtypeStruct(out_shape, F32),
        compiler_params=_params("parallel"),
        name="ffn",
    )(xt, gpre, wgu_t, wd_t, gpost)


def _rope_table_t(pos, n_dims, theta):
    inv = theta ** (-(jnp.arange(0, n_dims, 2, dtype=F32) / n_dims))
    ang = inv[:, None] * pos.astype(F32)[None, :]
    return jnp.cos(ang), jnp.sin(ang)


def kernel(x, mem, mem_norm_g, w_in, w_mem_kv, w_o, g_mix_pre, g_mix_post, attn_sink, qk_norm_g,
           w_gate_up, w_down, g_ffn_pre, g_ffn_post):
    batch, seq_len, _ = x.shape
    assert seq_len % (KEY_CHUNK * FLASH_CHUNKS) == 0 and seq_len % OUT_TILE == 0
    pos = jnp.arange(seq_len, dtype=jnp.int32)
    partial_tables = _rope_table_t(pos, ROPE_DIMS, ROPE_THETA)
    axial_tables = (_rope_table_t(pos // GRID_W, HEAD_DIM // 2, AXIAL_THETA)
                    + _rope_table_t(pos % GRID_W, HEAD_DIM // 2, AXIAL_THETA))

    w_in_t = jnp.swapaxes(w_in, 1, 2).astype(BF16)
    w_o_t = jnp.swapaxes(w_o, 1, 2).astype(BF16)
    w_gu_t = jnp.swapaxes(w_gate_up, 1, 2).astype(BF16)
    w_gu_t = jnp.swapaxes(w_gu_t.reshape(DEPTH, 2, D_FF // FF_BLOCK, FF_BLOCK, D_MODEL), 1, 2).reshape(
        DEPTH, 2 * D_FF, D_MODEL)
    w_d_t = jnp.swapaxes(w_down, 1, 2).astype(BF16)
    w_mem = w_mem_kv.astype(BF16).reshape(DEPTH, D_MODEL, 2, N_MEM_HEADS, HEAD_DIM)
    w_mk = jnp.transpose(w_mem[:, :, 0], (0, 2, 1, 3))
    w_mv_t = jnp.transpose(w_mem[:, :, 1], (0, 2, 3, 1))

    km, vmt = _mem_kv(mem, mem_norm_g.reshape(1, D_MODEL), w_mk, w_mv_t)

    xt = None
    for i in range(DEPTH):
        kind = i % N_MIXERS
        g_pre = g_mix_pre[i].reshape(D_MODEL, 1)
        if kind == 1:
            qk_g = qk_norm_g[i // N_MIXERS].reshape(2, HEAD_DIM, 1)
            qt, kt, vt, qmt = _in_proj(xt, g_pre, w_in_t[i], axial_tables, qk_g, seq_len)
            tok_t = _full_attention(qt, kt, vt, batch, seq_len)
        else:
            if i == 0:
                xt, qt, kt, vt, qmt = _in_proj(x.reshape(batch * seq_len, D_MODEL), g_pre.reshape(1, D_MODEL),
                                               w_in_t[i], partial_tables, None, seq_len, entry=True)
            else:
                qt, kt, vt, qmt = _in_proj(xt, g_pre, w_in_t[i], partial_tables, None, seq_len)
            if kind == 0:
                tok_t = _banded_attention(attn_sink[i // N_MIXERS], qt, kt, vt, batch, seq_len)
            else:
                tok_t = _dilated_attention(qt, kt, vt, batch, seq_len)
        xt = _out_proj(xt, tok_t, qmt, km[i], vmt[i], w_o_t[i], g_mix_post[i].reshape(D_MODEL, 1), seq_len)
        xt = _ffn(xt, g_ffn_pre[i].reshape(D_MODEL, 1), w_gu_t[i], w_d_t[i], g_ffn_post[i].reshape(D_MODEL, 1),
                  token_major_out=(i == DEPTH - 1))
    return xt.reshape(batch, seq_len, D_MODEL)
```
